```python
import jax, jax.numpy as jnp
from jax import lax
import numpy as np

D_MODEL = 1024
BATCH = 2
SEQ = 8192
DEPTH = 1
DEC_BATCH = 32
DEC_SEQ = 8
PAST_LEN = 8192
PAGE_SIZE = 128

HEAD_DIM = 64
N_HEADS = D_MODEL // HEAD_DIM
H_MOBA = N_HEADS // 2
H_FOX = N_HEADS - H_MOBA
W_MOBA = H_MOBA * HEAD_DIM
W_FOX = H_FOX * HEAD_DIM
W_MIX = W_MOBA + W_FOX
D_IN = 3 * W_MOBA + 3 * W_FOX + H_FOX
MOBA_BLOCK = 256
MOBA_TOPK = 3
MOBA_Q_CHUNK = 64
FOX_Q_CHUNK = 128
FORGET_BIAS_INIT = 2.0
ROPE_THETA = 500000.0
ROPE_DIM = HEAD_DIM // 4
PEER_HEADS = 8
PEER_NKEYS = 128
PEER_EXPERTS = PEER_NKEYS * PEER_NKEYS
PEER_QDIM = 256
PEER_HALF = PEER_QDIM // 2
PEER_TOPK = 16
PEER_TOK_BLOCK = 128
N_ADA = 6
EPS = 1e-6

kernel_name = "hymba_moba_fox_peer_step"

F32 = jnp.float32


def rmsnorm(x, g):
    xf = x.astype(F32)
    y = xf * lax.rsqrt(jnp.mean(xf * xf, axis=-1, keepdims=True) + EPS)
    return (y * g.astype(F32)).astype(x.dtype)


def modulation(c, w_ada, b_ada):
    m = jax.nn.silu(c) @ w_ada + b_ada
    return m.reshape(c.shape[0], N_ADA, D_MODEL)


def adaln(x, g, shift, scale):
    return rmsnorm(x, g) * (1.0 + scale[:, None, :]) + shift[:, None, :]


def rope_partial(x, pos):
    half = ROPE_DIM // 2
    inv = ROPE_THETA ** (-jnp.arange(half, dtype=F32) / half)
    ang = pos.astype(F32)[:, None] * inv[None, :]
    cos = jnp.cos(ang)[None, :, None, :]
    sin = jnp.sin(ang)[None, :, None, :]
    xr = x[..., :ROPE_DIM].astype(F32)
    x1, x2 = xr[..., :half], xr[..., half:]
    rot = jnp.concatenate([x1 * cos - x2 * sin, x2 * cos + x1 * sin], axis=-1).astype(x.dtype)
    return jnp.concatenate([rot, x[..., ROPE_DIM:]], axis=-1)


def mixer_inputs(h, w_in, b_forget, pos):
    B, S, _ = h.shape
    p = h @ w_in
    cuts = np.cumsum([W_MOBA, W_MOBA, W_MOBA, W_FOX, W_FOX, W_FOX]).tolist()
    qm, km, vm, qf, kf, vf, fl = jnp.split(p, cuts, axis=-1)
    hm = lambda a: a.reshape(B, S, H_MOBA, HEAD_DIM)
    hf = lambda a: a.reshape(B, S, H_FOX, HEAD_DIM)
    qm = rope_partial(hm(qm), pos)
    km = rope_partial(hm(km), pos)
    logf = jax.nn.log_sigmoid((fl + b_forget).astype(F32))
    return qm, km, hm(vm), hf(qf), hf(kf), hf(vf), logf


def moba_blocks(k, v):
    B, L, H, dh = k.shape
    nb = -(-L // MOBA_BLOCK)
    pad = nb * MOBA_BLOCK - L
    k = jnp.pad(k, ((0, 0), (0, pad), (0, 0), (0, 0)))
    v = jnp.pad(v, ((0, 0), (0, pad), (0, 0), (0, 0)))
    kb = k.reshape(B, nb, MOBA_BLOCK, H, dh).transpose(0, 3, 1, 2, 4)
    vb = v.reshape(B, nb, MOBA_BLOCK, H, dh).transpose(0, 3, 1, 2, 4)
    kmean = jnp.mean(kb.astype(F32), axis=3)
    return kb, vb, kmean


def moba_attend(q, q_pos, kb, vb, kmean):
    B, H, nb = kb.shape[:3]
    qblk = q_pos // MOBA_BLOCK
    gate = jnp.einsum('bhcd,bhnd->bhcn', q.astype(F32), kmean)
    fully_past = jnp.arange(nb)[None, :] < qblk[:, None]
    gate = jnp.where(fully_past[None, None], gate, -jnp.inf)
    _, top = lax.top_k(gate, min(MOBA_TOPK, nb))
    own = jnp.broadcast_to(qblk[None, None, :, None], top.shape[:3] + (1,))
    sel = jnp.concatenate([top, own], axis=-1)
    sel_ok = jnp.concatenate([top < qblk[None, None, :, None], jnp.ones(own.shape, bool)], axis=-1)
    bi = jnp.arange(B)[:, None, None, None]
    hi = jnp.arange(H)[None, :, None, None]
    kg = kb[bi, hi, sel]
    vg = vb[bi, hi, sel]
    s = jnp.einsum('bhcd,bhcnjd->bhcnj', q, kg).astype(F32) * (HEAD_DIM ** -0.5)
    key_pos = sel[..., None] * MOBA_BLOCK + jnp.arange(MOBA_BLOCK)
    ok = sel_ok[..., None] & (key_pos <= q_pos[None, None, :, None, None])
    s = jnp.where(ok, s, -jnp.inf)
    p = jax.nn.softmax(s.reshape(s.shape[:3] + (-1,)), axis=-1).reshape(s.shape)
    return jnp.einsum('bhcnj,bhcnjd->bhcd', p.astype(vg.dtype), vg)


def moba_prompt(q, k, v):
    B, S, H, dh = q.shape
    kb, vb, kmean = moba_blocks(k, v)
    nc = S // MOBA_Q_CHUNK
    qc = q.reshape(B, nc, MOBA_Q_CHUNK, H, dh).transpose(1, 0, 3, 2, 4)
    pc = jnp.arange(S, dtype=jnp.int32).reshape(nc, MOBA_Q_CHUNK)
    o = lax.map(lambda a: moba_attend(a[0], a[1], kb, vb, kmean), (qc, pc))
    return o.transpose(1, 0, 3, 2, 4).reshape(B, S, H * dh)


def moba_sample(q, q_pos, k_all, v_all):
    B, T, H, dh = q.shape
    kb, vb, kmean = moba_blocks(k_all, v_all)
    o = moba_attend(q.transpose(0, 2, 1, 3), q_pos, kb, vb, kmean)
    return o.transpose(0, 2, 1, 3).reshape(B, T, H * dh)


def fox_attend(q, q_pos, fq, k, v, fk, k_pos):
    s = jnp.einsum('bhcd,bhld->bhcl', q, k).astype(F32) * (HEAD_DIM ** -0.5)
    s = s + (fq[..., :, None] - fk[..., None, :])
    s = jnp.where((k_pos[None, :] <= q_pos[:, None])[None, None], s, -jnp.inf)
    p = jax.nn.softmax(s, axis=-1)
    return jnp.einsum('bhcl,bhld->bhcd', p.astype(v.dtype), v)


def fox_prompt(q, k, v, logf):
    B, S, H, dh = q.shape
    F = jnp.cumsum(logf.astype(F32), axis=1).transpose(0, 2, 1)
    kh = k.transpose(0, 2, 1, 3)
    vh = v.transpose(0, 2, 1, 3)
    nc = S // FOX_Q_CHUNK
    qc = q.reshape(B, nc, FOX_Q_CHUNK, H, dh).transpose(1, 0, 3, 2, 4)
    fc = F.reshape(B, H, nc, FOX_Q_CHUNK).transpose(2, 0, 1, 3)
    pos = jnp.arange(S, dtype=jnp.int32)
    pc = pos.reshape(nc, FOX_Q_CHUNK)
    o = lax.map(lambda a: fox_attend(a[0], a[1], a[2], kh, vh, F, pos), (qc, pc, fc))
    return o.transpose(1, 0, 3, 2, 4).reshape(B, S, H * dh)


def fox_sample(q, q_pos, k_all, v_all, logf_all, past_len):
    B, T, H, dh = q.shape
    F = jnp.cumsum(logf_all.astype(F32), axis=1).transpose(0, 2, 1)
    k_pos = jnp.arange(k_all.shape[1], dtype=jnp.int32)
    o = fox_attend(q.transpose(0, 2, 1, 3), q_pos, F[:, :, past_len:],
                   k_all.transpose(0, 2, 1, 3), v_all.transpose(0, 2, 1, 3), F, k_pos)
    return o.transpose(0, 2, 1, 3).reshape(B, T, H * dh)


def merge_heads(om, of, g_m, g_f, w_out):
    return jnp.concatenate([rmsnorm(om, g_m), rmsnorm(of, g_f)], axis=-1) @ w_out


def gather_pages(pool, page_table):
    g = pool[page_table]
    return g.reshape((g.shape[0], g.shape[1] * g.shape[2]) + g.shape[3:])


def peer_block(h, wq, subkeys, u_tab, v_tab):
    n = h.shape[0]
    q = (h @ wq).reshape(n, PEER_HEADS, 2, PEER_HALF).astype(F32)
    s = jnp.einsum('nhpd,hpkd->nhpk', q, subkeys.astype(F32))
    sv, si = lax.top_k(s, PEER_TOPK)
    cand = (sv[:, :, 0, :, None] + sv[:, :, 1, None, :]).reshape(n, PEER_HEADS, -1)
    cid = (si[:, :, 0, :, None] * PEER_NKEYS + si[:, :, 1, None, :]).reshape(n, PEER_HEADS, -1)
    top_s, top_j = lax.top_k(cand, PEER_TOPK)
    eid = jnp.take_along_axis(cid, top_j, axis=-1)
    gate = jax.nn.softmax(top_s, axis=-1)
    act = jax.nn.gelu(jnp.einsum('nd,nhkd->nhk', h, u_tab[eid]).astype(F32), approximate=False)
    return jnp.einsum('nhk,nhkd->nd', (gate * act).astype(h.dtype), v_tab[eid])


def peer_ffn(h, wq, subkeys, u_tab, v_tab):
    B, S, D = h.shape
    n = B * S
    nb = -(-n // PEER_TOK_BLOCK)
    hf = jnp.pad(h.reshape(n, D), ((0, nb * PEER_TOK_BLOCK - n), (0, 0))).reshape(nb, PEER_TOK_BLOCK, D)
    out = lax.map(lambda hb: peer_block(hb, wq, subkeys, u_tab, v_tab), hf)
    return out.reshape(nb * PEER_TOK_BLOCK, D)[:n].reshape(B, S, D)


def setup_inputs(seed: int = 0) -> dict:
    key = jax.random.key(seed)
    ks = jax.random.split(key, 24)
    n_pages = PAST_LEN // PAGE_SIZE
    n_used = DEC_BATCH * n_pages
    n_pool = n_used + n_used // 4

    def nrm(k, shape, s=1.0):
        return s * jax.random.normal(k, shape, F32)

    return {
        "x_prompt": nrm(ks[0], (BATCH, SEQ, D_MODEL)),
        "x_sample": nrm(ks[1], (DEC_BATCH, DEC_SEQ, D_MODEL)),
        "c_prompt": nrm(ks[2], (BATCH, D_MODEL)),
        "c_sample": nrm(ks[3], (DEC_BATCH, D_MODEL)),
        "cache_k_moba": nrm(ks[4], (DEPTH, n_pool, PAGE_SIZE, H_MOBA, HEAD_DIM)),
        "cache_v_moba": nrm(ks[5], (DEPTH, n_pool, PAGE_SIZE, H_MOBA, HEAD_DIM)),
        "cache_k_fox": nrm(ks[6], (DEPTH, n_pool, PAGE_SIZE, H_FOX, HEAD_DIM)),
        "cache_v_fox": nrm(ks[7], (DEPTH, n_pool, PAGE_SIZE, H_FOX, HEAD_DIM)),
        "cache_logf_fox": jax.nn.log_sigmoid(FORGET_BIAS_INIT + nrm(ks[8], (DEPTH, n_pool, PAGE_SIZE, H_FOX))),
        "page_table": jax.random.permutation(ks[9], n_pool)[:n_used].reshape(DEC_BATCH, n_pages).astype(jnp.int32),
        "w_ada": nrm(ks[10], (DEPTH, D_MODEL, N_ADA * D_MODEL), 0.5 * D_MODEL ** -0.5),
        "b_ada": nrm(ks[11], (DEPTH, N_ADA * D_MODEL), 0.01),
        "g_attn": 1.0 + nrm(ks[12], (DEPTH, D_MODEL), 0.05),
        "g_ffn": 1.0 + nrm(ks[13], (DEPTH, D_MODEL), 0.05),
        "w_in": nrm(ks[14], (DEPTH, D_MODEL, D_IN), D_MODEL ** -0.5),
        "b_forget": FORGET_BIAS_INIT + nrm(ks[15], (DEPTH, H_FOX), 0.1),
        "g_out_moba": 1.0 + nrm(ks[16], (DEPTH, W_MOBA), 0.05),
        "g_out_fox": 1.0 + nrm(ks[17], (DEPTH, W_FOX), 0.05),
        "w_out": nrm(ks[18], (DEPTH, W_MIX, D_MODEL), W_MIX ** -0.5),
        "peer_wq": nrm(ks[19], (DEPTH, D_MODEL, PEER_HEADS * PEER_QDIM), D_MODEL ** -0.5),
        "peer_subkeys": nrm(ks[20], (DEPTH, PEER_HEADS, 2, PEER_NKEYS, PEER_HALF), PEER_HALF ** -0.5),
        "peer_u": nrm(ks[21], (DEPTH, PEER_EXPERTS, D_MODEL), D_MODEL ** -0.5),
        "peer_v": nrm(ks[22], (DEPTH, PEER_EXPERTS, D_MODEL), PEER_HEADS ** -0.5),
        "g_final": 1.0 + nrm(ks[23], (D_MODEL,), 0.05),
    }


def reference(x_prompt, x_sample, c_prompt, c_sample, cache_k_moba, cache_v_moba, cache_k_fox,
              cache_v_fox, cache_logf_fox, page_table, w_ada, b_ada, g_attn, g_ffn, w_in, b_forget,
              g_out_moba, g_out_fox, w_out, peer_wq, peer_subkeys, peer_u, peer_v, g_final):
    S = x_prompt.shape[1]
    T = x_sample.shape[1]
    past_len = page_table.shape[1] * cache_k_moba.shape[2]
    pos_p = jnp.arange(S, dtype=jnp.int32)
    pos_s = past_len + jnp.arange(T, dtype=jnp.int32)
    xp, xs = x_prompt, x_sample
    kmp, vmp, kfp, vfp, lfp = [], [], [], [], []
    kms, vms, kfs, vfs, lfs = [], [], [], [], []
    for l in range(DEPTH):
        mp = modulation(c_prompt, w_ada[l], b_ada[l])
        ms = modulation(c_sample, w_ada[l], b_ada[l])

        qm, km, vm, qf, kf, vf, lf = mixer_inputs(adaln(xp, g_attn[l], mp[:, 0], mp[:, 1]), w_in[l], b_forget[l], pos_p)
        om = moba_prompt(qm, km, vm)
        of = fox_prompt(qf, kf, vf, lf)
        xp = xp + mp[:, 2, None, :] * merge_heads(om, of, g_out_moba[l], g_out_fox[l], w_out[l])
        kmp.append(km); vmp.append(vm); kfp.append(kf); vfp.append(vf); lfp.append(lf)

        qm, km, vm, qf, kf, vf, lf = mixer_inputs(adaln(xs, g_attn[l], ms[:, 0], ms[:, 1]), w_in[l], b_forget[l], pos_s)
        km_all = jnp.concatenate([gather_pages(cache_k_moba[l], page_table), km], axis=1)
        vm_all = jnp.concatenate([gather_pages(cache_v_moba[l], page_table), vm], axis=1)
        kf_all = jnp.concatenate([gather_pages(cache_k_fox[l], page_table), kf], axis=1)
        vf_all = jnp.concatenate([gather_pages(cache_v_fox[l], page_table), vf], axis=1)
        lf_all = jnp.concatenate([gather_pages(cache_logf_fox[l], page_table).astype(F32), lf], axis=1)
        om = moba_sample(qm, pos_s, km_all, vm_all)
        of = fox_sample(qf, pos_s, kf_all, vf_all, lf_all, past_len)
        xs = xs + ms[:, 2, None, :] * merge_heads(om, of, g_out_moba[l], g_out_fox[l], w_out[l])
        kms.append(km); vms.append(vm); kfs.append(kf); vfs.append(vf); lfs.append(lf)

        xp = xp + mp[:, 5, None, :] * peer_ffn(adaln(xp, g_ffn[l], mp[:, 3], mp[:, 4]), peer_wq[l], peer_subkeys[l], peer_u[l], peer_v[l])
        xs = xs + ms[:, 5, None, :] * peer_ffn(adaln(xs, g_ffn[l], ms[:, 3], ms[:, 4]), peer_wq[l], peer_subkeys[l], peer_u[l], peer_v[l])

    y_prompt = rmsnorm(xp, g_final)
    y_sample = rmsnorm(xs, g_final)
    return (y_prompt, y_sample,
            jnp.stack(kmp), jnp.stack(vmp), jnp.stack(kfp), jnp.stack(vfp), jnp.stack(lfp),
            jnp.stack(kms), jnp.stack(vms), jnp.stack(kfs), jnp.stack(vfs), jnp.stack(lfs))
```

```python
import functools
import math

import numpy as np
import jax
import jax.numpy as jnp
from jax import lax
from jax.experimental import pallas as pl
from jax.experimental.pallas import tpu as pltpu

F32 = jnp.float32
BF16 = jnp.bfloat16
HIGHEST = lax.Precision.HIGHEST

HEAD_DIM = 64
N_GROUP_HEADS = 8
GROUP_W = N_GROUP_HEADS * HEAD_DIM
MOBA_BLOCK = 256
MOBA_BLOCK_LOG2 = 8
MOBA_TOPK = 3
ROPE_THETA = 500000.0
ROPE_DIM = HEAD_DIM // 4
ROPE_HALF = ROPE_DIM // 2
PEER_HEADS = 8
PEER_NKEYS = 128
PEER_HALF = 128
PEER_TOPK = 16
N_ADA = 6
EPS = 1e-6

LANES = 128
SUBLANES = 8
VMEM_LIMIT_BYTES = 56 * 1024 * 1024

LOG2E = math.log2(math.e)
Q_SCALE = HEAD_DIM ** -0.5 * LOG2E
NEG_BIG = -1.0e30
PAD_W = 2 * HEAD_DIM
FOX_EXT_ROWS = 16
MOBA_EXT_ROWS = 32


def _cparams(sem):
    return pltpu.CompilerParams(dimension_semantics=sem, vmem_limit_bytes=VMEM_LIMIT_BYTES)


def _nt_dot(a, b):
    return lax.dot_general(a, b, (((1,), (1,)), ((), ())), preferred_element_type=F32)


def _split3(x):
    hi = x.astype(BF16)
    r1 = x - hi.astype(F32)
    mid = r1.astype(BF16)
    lo = (r1 - mid.astype(F32)).astype(BF16)
    return hi, mid, lo


def _dot3(a01, x):
    hi, mid, lo = _split3(x)
    d = functools.partial(jnp.dot, preferred_element_type=F32)
    return d(a01, hi) + d(a01, mid) + d(a01, lo)


def _log_sigmoid(z):
    return jnp.minimum(z, 0.0) - jnp.log1p(jnp.exp(-jnp.abs(z)))


def _rms_scale(x):
    return lax.rsqrt(jnp.mean(x * x, axis=-1, keepdims=True) + EPS)


def _mod_kernel(c_ref, w_ref, b_ref, o_ref):
    c = c_ref[...]
    s = c / (1.0 + jnp.exp(-c))
    o_ref[...] = jnp.dot(s, w_ref[...], precision=HIGHEST, preferred_element_type=F32) + b_ref[...]


def _modulation(c_all, w_ada, b_ada):
    m, d = c_all.shape
    n = w_ada.shape[1]
    tn = 1536
    return pl.pallas_call(
        _mod_kernel,
        grid=(n // tn,),
        in_specs=[pl.BlockSpec((m, d), lambda j: (0, 0)),
                  pl.BlockSpec((d, tn), lambda j: (0, j)),
                  pl.BlockSpec((1, tn), lambda j: (0, j))],
        out_specs=pl.BlockSpec((m, tn), lambda j: (0, j)),
        out_shape=jax.ShapeDtypeStruct((m, n), F32),
        compiler_params=_cparams(("arbitrary",)),
        name="modulation",
    )(c_all, w_ada, b_ada.reshape(1, n))


def _rope_tables(pos):
    inv = ROPE_THETA ** (-jnp.arange(ROPE_HALF, dtype=F32) / ROPE_HALF)
    ang = pos.astype(F32)[:, None] * inv[None, :]
    cos, sin = jnp.cos(ang), jnp.sin(ang)
    t = pos.shape[0]
    one = jnp.ones((t, HEAD_DIM - ROPE_DIM), F32)
    zero = jnp.zeros((t, HEAD_DIM - ROPE_DIM), F32)
    z8 = jnp.zeros((t, ROPE_HALF), F32)
    c = jnp.concatenate([cos, cos, one], axis=1)
    sm = jnp.concatenate([-sin, z8, zero], axis=1)
    sp = jnp.concatenate([z8, sin, zero], axis=1)
    nat = jnp.stack([jnp.tile(a, (1, 2)) for a in (c, sm, sp)])
    tr = jnp.stack([cos.T, sin.T])
    return nat, tr


def _rope_nat(x, tab_ref):
    xm = pltpu.roll(x, LANES - ROPE_HALF, axis=1)
    xp = pltpu.roll(x, ROPE_HALF, axis=1)
    return x * tab_ref[0] + xm * tab_ref[1] + xp * tab_ref[2]


def _inproj_kernel(x_ref, shift_ref, scale_ref, g_ref, wnat_ref, wt_ref, bf_ref, tab_ref, tabt_ref,
                   ltri_ref, pk_ref, pq_ref,
                   km_ref, vm_ref, kf_ref, vf_ref, logf_ref, kpm_ref, kpf_ref,
                   qtm_ref, qtg_ref, qtf_ref, fqt_ref, vtm_ref, vtf_ref, kmean_ref,
                   carry_ref, *, ts):
    j = pl.program_id(1)
    nh = N_GROUP_HEADS
    x = x_ref[...]
    h = (x * _rms_scale(x) * g_ref[...]) * (1.0 + scale_ref[...]) + shift_ref[...]
    hb = h.astype(BF16)
    pn = jnp.dot(hb, wnat_ref[...], preferred_element_type=F32)
    pt = _nt_dot(wt_ref[...], hb)
    w_sp = nh * PAD_W
    kpm = pn[:, 0:w_sp]
    kpf = pn[:, w_sp:2 * w_sp]
    vm_ref[...] = pn[:, 2 * w_sp:2 * w_sp + GROUP_W]
    vf_ref[...] = pn[:, 2 * w_sp + GROUP_W:2 * w_sp + 2 * GROUP_W]
    fl = pn[:, 2 * w_sp + 2 * GROUP_W:2 * w_sp + 2 * GROUP_W + LANES]

    lane = lax.broadcasted_iota(jnp.int32, (ts, PAD_W), 1)
    row = lax.broadcasted_iota(jnp.int32, (ts, PAD_W), 0)
    blk = jnp.right_shift(j * ts + row, MOBA_BLOCK_LOG2)
    onehot = lane == HEAD_DIM + blk
    km_heads = [_rope_nat(kpm[:, hh * PAD_W:(hh + 1) * PAD_W], tab_ref) for hh in range(nh)]
    nblk = ts // MOBA_BLOCK
    for hh in range(nh):
        kh = km_heads[hh]
        kpm_ref[:, hh * PAD_W:(hh + 1) * PAD_W] = jnp.where(onehot, 1.0, kh).astype(BF16)
        for bb in range(nblk):
            kmean_ref[bb, :, hh * PAD_W:(hh + 1) * PAD_W] = jnp.mean(
                kh[bb * MOBA_BLOCK:(bb + 1) * MOBA_BLOCK], axis=0, keepdims=True)
    for pp in range(nh // 2):
        km_ref[:, pp * PAD_W:(pp + 1) * PAD_W] = (
            km_heads[2 * pp] + pltpu.roll(km_heads[2 * pp + 1], HEAD_DIM, axis=1))
        kf_ref[:, pp * PAD_W:(pp + 1) * PAD_W] = (
            kpf[:, 2 * pp * PAD_W:(2 * pp + 1) * PAD_W]
            + pltpu.roll(kpf[:, (2 * pp + 1) * PAD_W:(2 * pp + 2) * PAD_W], HEAD_DIM, axis=1))

    lane1 = lax.broadcasted_iota(jnp.int32, (ts, LANES), 1)
    logf = jnp.where(lane1 < nh, _log_sigmoid(fl + bf_ref[...]), 0.0)
    logf_ref[...] = logf

    @pl.when(j == 0)
    def _():
        carry_ref[...] = jnp.zeros_like(carry_ref)

    cum = _dot3(ltri_ref[...], logf) + carry_ref[0:1, :]
    carry_ref[0:1, :] = cum[ts - 1:ts, :]
    g = cum * LOG2E
    g_hi, g_mid, g_lo = _split3(g)
    parts = (g_hi.astype(F32) + pltpu.roll(g_mid.astype(F32), nh, axis=1)
             + pltpu.roll(g_lo.astype(F32), 2 * nh, axis=1)
             + jnp.where(lane1 == 3 * nh, 1.0, 0.0)).astype(BF16)
    kpf_ref[...] = (kpf + jnp.dot(parts, pk_ref[...], preferred_element_type=F32)).astype(BF16)
    fqt_ref[...] = _nt_dot(pq_ref[...], parts).astype(BF16)

    qtm = pt[0:GROUP_W]
    cos_t = tabt_ref[0]
    sin_t = tabt_ref[1]
    pieces = []
    for hh in range(nh):
        base = hh * HEAD_DIM
        x1 = qtm[base:base + ROPE_HALF]
        x2 = qtm[base + ROPE_HALF:base + ROPE_DIM]
        pieces += [x1 * cos_t - x2 * sin_t, x2 * cos_t + x1 * sin_t, qtm[base + ROPE_DIM:base + HEAD_DIM]]
    qtm = jnp.concatenate(pieces, axis=0)
    qtg_ref[...] = qtm
    qtm_ref[...] = qtm.astype(BF16)
    qtf_ref[...] = pt[GROUP_W:2 * GROUP_W].astype(BF16)
    vtm = pt[2 * GROUP_W:3 * GROUP_W].astype(BF16)
    vtf = pt[3 * GROUP_W:4 * GROUP_W].astype(BF16)
    for bb in range(nblk):
        sl = slice(bb * MOBA_BLOCK, (bb + 1) * MOBA_BLOCK)
        for hh in range(nh):
            vtm_ref[hh, bb] = vtm[hh * HEAD_DIM:(hh + 1) * HEAD_DIM, sl]
            vtf_ref[hh, bb] = vtf[hh * HEAD_DIM:(hh + 1) * HEAD_DIM, sl]


def _spread_heads(w):
    d = w.shape[0]
    w = w.reshape(d, N_GROUP_HEADS, HEAD_DIM)
    return jnp.pad(w, ((0, 0), (0, 0), (0, PAD_W - HEAD_DIM))).reshape(d, N_GROUP_HEADS * PAD_W)


def _split_w_in(w_in):
    cuts = np.cumsum([GROUP_W] * 6).tolist()
    return jnp.split(w_in, cuts, axis=1)


def _fox_placement():
    nh = N_GROUP_HEADS
    pk = np.zeros((LANES, nh * PAD_W), np.float32)
    pq = np.zeros((nh * FOX_EXT_ROWS, LANES), np.float32)
    for hh in range(nh):
        for t in range(3):
            pk[t * nh + hh, hh * PAD_W + HEAD_DIM + t] = -1.0
            pk[3 * nh, hh * PAD_W + HEAD_DIM + 3 + t] = 1.0
            pq[hh * FOX_EXT_ROWS + t, 3 * nh] = 1.0
            pq[hh * FOX_EXT_ROWS + 3 + t, t * nh + hh] = 1.0
    return jnp.asarray(pk, BF16), jnp.asarray(pq, BF16)


def _inproj_prompt(x, shift, scale, g_attn, w_in, b_forget, ts):
    b, s, d = x.shape
    nh = N_GROUP_HEADS
    nb = s // MOBA_BLOCK
    wqm, wkm, wvm, wqf, wkf, wvf, wfl = _split_w_in(w_in)
    wnat = jnp.concatenate([_spread_heads(wkm), _spread_heads(wkf), wvm, wvf,
                            jnp.pad(wfl, ((0, 0), (0, LANES - nh)))], axis=1).astype(BF16)
    wt = jnp.concatenate([wqm * Q_SCALE, wqf * Q_SCALE, wvm, wvf], axis=1).T.astype(BF16)
    bf = jnp.pad(b_forget, (0, LANES - nh)).reshape(1, LANES)
    tab, tabt = _rope_tables(jnp.arange(s, dtype=jnp.int32))
    ltri = jnp.tril(jnp.ones((ts, ts), F32)).astype(BF16)
    pk, pq = _fox_placement()
    nblk = ts // MOBA_BLOCK
    w_sp = nh * PAD_W

    def tok(width, dtype):
        return (jax.ShapeDtypeStruct((b, s, width), dtype),
                pl.BlockSpec((None, ts, width), lambda i, j: (i, j, 0)))

    def tr(rows, dtype):
        return (jax.ShapeDtypeStruct((b, rows, s), dtype),
                pl.BlockSpec((None, rows, ts), lambda i, j: (i, 0, j)))

    vt = (jax.ShapeDtypeStruct((b, nh, nb, HEAD_DIM, MOBA_BLOCK), BF16),
          pl.BlockSpec((None, nh, nblk, HEAD_DIM, MOBA_BLOCK), lambda i, j: (i, 0, j, 0, 0)))
    kmean = (jax.ShapeDtypeStruct((b, nb, 1, w_sp), F32),
             pl.BlockSpec((None, nblk, 1, w_sp), lambda i, j: (i, j, 0, 0)))
    outs = [tok(GROUP_W, F32), tok(GROUP_W, F32), tok(GROUP_W, F32), tok(GROUP_W, F32),
            tok(LANES, F32), tok(w_sp, BF16), tok(w_sp, BF16),
            tr(GROUP_W, BF16), tr(GROUP_W, F32), tr(GROUP_W, BF16), tr(nh * FOX_EXT_ROWS, BF16),
            vt, vt, kmean]
    const = lambda shape: pl.BlockSpec(shape, lambda i, j: (0,) * len(shape))
    row_spec = pl.BlockSpec((None, 1, d), lambda i, j: (i, 0, 0))
    return pl.pallas_call(
        functools.partial(_inproj_kernel, ts=ts),
        grid=(b, s // ts),
        in_specs=[pl.BlockSpec((None, ts, d), lambda i, j: (i, j, 0)), row_spec, row_spec,
                  const((1, d)), const(wnat.shape), const(wt.shape), const((1, LANES)),
                  pl.BlockSpec((3, ts, LANES), lambda i, j: (0, j, 0)),
                  pl.BlockSpec((2, ROPE_HALF, ts), lambda i, j: (0, 0, j)),
                  const((ts, ts)), const(pk.shape), const(pq.shape)],
        out_specs=[o[1] for o in outs],
        out_shape=[o[0] for o in outs],
        scratch_shapes=[pltpu.VMEM((SUBLANES, LANES), F32)],
        compiler_params=_cparams(("arbitrary", "arbitrary")),
        name="inproj_prompt",
    )(x, shift, scale, g_attn.reshape(1, d), wnat, wt, bf, tab, tabt, ltri, pk, pq)


def _first_max(g, idx, big):
    mx = jnp.max(g, axis=0, keepdims=True)
    first = jnp.min(jnp.where(g == mx, idx, big), axis=0, keepdims=True)
    return mx, first


def _moba_gate_kernel(qt_ref, kmean_ref, bias_ref, *, tg, nb):
    i = pl.program_id(2)
    gate = jnp.dot(kmean_ref[...][:, 0:HEAD_DIM], qt_ref[...], precision=HIGHEST,
                   preferred_element_type=F32)
    n = lax.broadcasted_iota(jnp.int32, (nb, tg), 0)
    qblk = jnp.right_shift(i * tg + lax.broadcasted_iota(jnp.int32, (nb, tg), 1), MOBA_BLOCK_LOG2)
    past = n < qblk
    g = jnp.where(past, gate, -jnp.inf)
    sel = n == qblk
    for _ in range(MOBA_TOPK):
        _, first = _first_max(g, n, nb)
        pick = n == first
        sel = sel | (pick & past)
        g = jnp.where(pick, -jnp.inf, g)
    bias_ref[...] = jnp.where(sel, 0.0, NEG_BIG).astype(BF16)


def _moba_gate(qtg, kmean, tg):
    b, _, s = qtg.shape
    nb = MOBA_EXT_ROWS
    kmean = kmean.reshape(b, kmean.shape[1], N_GROUP_HEADS * PAD_W)
    kmean = jnp.pad(kmean, ((0, 0), (0, nb - kmean.shape[1]), (0, 0)))
    return pl.pallas_call(
        functools.partial(_moba_gate_kernel, tg=tg, nb=nb),
        grid=(b, N_GROUP_HEADS, s // tg),
        in_specs=[pl.BlockSpec((None, HEAD_DIM, tg), lambda bi, h, i: (bi, h, i)),
                  pl.BlockSpec((None, nb, PAD_W), lambda bi, h, i: (bi, 0, h))],
        out_specs=pl.BlockSpec((None, None, nb, tg), lambda bi, h, i: (bi, h, 0, i)),
        out_shape=jax.ShapeDtypeStruct((b, N_GROUP_HEADS, nb, s), BF16),
        compiler_params=_cparams(("arbitrary", "arbitrary", "arbitrary")),
        name="moba_gate",
    )(qtg, kmean)


def _attn_kernel(qt_ref, ext_ref, kp_ref, vt_ref, o_ref, qp_ref, *, tq, ext_rows):
    i = pl.program_id(2)
    qp_ref[0:HEAD_DIM, :] = qt_ref[...]
    qp_ref[HEAD_DIM:HEAD_DIM + ext_rows, :] = ext_ref[...]
    qp_ref[HEAD_DIM + ext_rows:PAD_W, :] = jnp.zeros((PAD_W - HEAD_DIM - ext_rows, tq), BF16)
    qp = qp_ref[...]

    kd = kp_ref[pl.ds(pl.multiple_of(i * tq, tq), tq), :]
    st = jnp.dot(kd, qp, preferred_element_type=F32)
    kpos = lax.broadcasted_iota(jnp.int32, (tq, tq), 0)
    qpos = lax.broadcasted_iota(jnp.int32, (tq, tq), 1)
    st = jnp.where(kpos <= qpos, st, -jnp.inf)
    m0 = jnp.max(st, axis=0, keepdims=True)
    p = jnp.exp2(st - m0)
    l0 = jnp.sum(p, axis=0, keepdims=True)
    acc0 = jnp.dot(vt_ref[i], p.astype(BF16), preferred_element_type=F32)

    def body(n, carry):
        m, l, acc = carry
        kn = kp_ref[pl.ds(pl.multiple_of(n * tq, tq), tq), :]
        s_t = jnp.dot(kn, qp, preferred_element_type=F32)
        m_new = jnp.maximum(m, jnp.max(s_t, axis=0, keepdims=True))
        alpha = jnp.exp2(m - m_new)
        pn = jnp.exp2(s_t - m_new)
        l = alpha * l + jnp.sum(pn, axis=0, keepdims=True)
        acc = alpha * acc + jnp.dot(vt_ref[n], pn.astype(BF16), preferred_element_type=F32)
        return m_new, l, acc

    _, l, acc = lax.fori_loop(0, i, body, (m0, l0, acc0))
    o_ref[...] = acc / l


def _attention(qt, ext, kp, vt, ext_rows, tq):
    b, _, s = qt.shape
    nh = N_GROUP_HEADS
    nb = s // tq
    if ext.ndim == 4:
        ext_spec = pl.BlockSpec((None, None, ext_rows, tq), lambda bi, h, i: (bi, h, 0, i))
    else:
        ext_spec = pl.BlockSpec((None, ext_rows, tq), lambda bi, h, i: (bi, h, i))
    return pl.pallas_call(
        functools.partial(_attn_kernel, tq=tq, ext_rows=ext_rows),
        grid=(b, nh, nb),
        in_specs=[pl.BlockSpec((None, HEAD_DIM, tq), lambda bi, h, i: (bi, h, i)),
                  ext_spec,
                  pl.BlockSpec((None, s, PAD_W), lambda bi, h, i: (bi, 0, h)),
                  pl.BlockSpec((None, None, nb, HEAD_DIM, tq), lambda bi, h, i: (bi, h, 0, 0, 0))],
        out_specs=pl.BlockSpec((None, HEAD_DIM, tq), lambda bi, h, i: (bi, h, i)),
        out_shape=jax.ShapeDtypeStruct((b, GROUP_W, s), F32),
        scratch_shapes=[pltpu.VMEM((PAD_W, tq), BF16)],
        compiler_params=_cparams(("arbitrary", "arbitrary", "arbitrary")),
        name="attention",
    )(qt, ext, kp, vt)


def _merge_kernel(x_ref, otm_ref, otf_ref, gate_ref, shift_ref, scale_ref, gm_ref, gf_ref, gffn_ref,
                  wout_ref, x1_ref, h2_ref):
    def group_norm(ot, gcol):
        r = lax.rsqrt(jnp.mean(ot * ot, axis=0, keepdims=True) + EPS)
        return (ot * r * gcol).T

    on = jnp.concatenate([group_norm(otm_ref[...], gm_ref[...]),
                          group_norm(otf_ref[...], gf_ref[...])], axis=1).astype(BF16)
    y = jnp.dot(on, wout_ref[...], preferred_element_type=F32)
    x1 = x_ref[...] + gate_ref[...] * y
    x1_ref[...] = x1
    h2 = (x1 * _rms_scale(x1) * gffn_ref[...]) * (1.0 + scale_ref[...]) + shift_ref[...]
    h2_ref[...] = h2.astype(BF16)


def _mod_spec(arr, tm, d):
    if arr.shape[1] == 1:
        return pl.BlockSpec((None, 1, d), lambda i, j: (i, 0, 0))
    return pl.BlockSpec((None, tm, d), lambda i, j: (i, j, 0))


def _merge(x, otm, otf, gate, shift, scale, g_m, g_f, g_ffn, w_out, tm):
    b, s, d = x.shape
    tokspec = pl.BlockSpec((None, tm, d), lambda i, j: (i, j, 0))
    otspec = pl.BlockSpec((None, GROUP_W, tm), lambda i, j: (i, 0, j))
    const = lambda shape: pl.BlockSpec(shape, lambda i, j: (0,) * len(shape))
    return pl.pallas_call(
        _merge_kernel,
        grid=(b, s // tm),
        in_specs=[tokspec, otspec, otspec, _mod_spec(gate, tm, d), _mod_spec(shift, tm, d),
                  _mod_spec(scale, tm, d), const((GROUP_W, 1)), const((GROUP_W, 1)), const((1, d)),
                  const((2 * GROUP_W, d))],
        out_specs=[tokspec, tokspec],
        out_shape=[jax.ShapeDtypeStruct((b, s, d), F32), jax.ShapeDtypeStruct((b, s, d), BF16)],
        compiler_params=_cparams(("arbitrary", "arbitrary")),
        name="merge_outproj",
    )(x, otm, otf, gate, shift, scale, g_m.reshape(GROUP_W, 1), g_f.reshape(GROUP_W, 1),
      g_ffn.reshape(1, d), w_out.astype(BF16))


_CAND_GROUPS = [(0, 16), (1, 8), (2, 5), (3, 4), (4, 3), (5, 2), (6, 2), (7, 2)]
_CAND_ROWS = 16 + 7 * 8 + 8


def _top16(s, tt):
    idx = lax.broadcasted_iota(jnp.int32, (PEER_NKEYS, tt), 0)
    r16 = lax.broadcasted_iota(jnp.int32, (PEER_TOPK, tt), 0)

    def body(r, carry):
        g, rank, sv = carry
        mx, first = _first_max(g, idx, PEER_NKEYS)
        pick = idx == first
        rank = jnp.where(pick, r.astype(F32), rank)
        g = jnp.where(pick, -jnp.inf, g)
        sv = jnp.where(r16 == r, mx, sv)
        return g, rank, sv

    _, rank, sv = lax.fori_loop(
        0, PEER_TOPK, body,
        (s, jnp.full((PEER_NKEYS, tt), float(PEER_TOPK), F32), jnp.zeros((PEER_TOPK, tt), F32)))
    return sv, rank


def _peer_route_kernel(h_ref, wqt_ref, sub_ref, rank1_ref, e1_ref, lim0_ref, e0_ref, *, tt):
    qt = _nt_dot(wqt_ref[...], h_ref[...])
    r16 = lax.broadcasted_iota(jnp.int32, (PEER_TOPK, tt), 0)
    crow = lax.broadcasted_iota(jnp.int32, (_CAND_ROWS, tt), 0)
    for hh in range(PEER_HEADS):
        s = []
        for p in range(2):
            base = (hh * 2 + p) * PEER_HALF
            s.append(jnp.dot(sub_ref[hh * 2 + p], qt[base:base + PEER_HALF], precision=HIGHEST,
                             preferred_element_type=F32))
        sv0, rank0 = _top16(s[0], tt)
        sv1, rank1 = _top16(s[1], tt)

        groups = []
        for a, cnt in _CAND_GROUPS:
            rows = 16 if a == 0 else SUBLANES
            blk = sv0[a:a + 1] + sv1[0:rows]
            if cnt < rows:
                blk = jnp.where(lax.broadcasted_iota(jnp.int32, (rows, tt), 0) < cnt, blk, -jnp.inf)
            groups.append(blk)
        groups.append(sv0[8:16] + sv1[0:1])
        cand = jnp.concatenate(groups, axis=0)
        top = sv0[0:1] + sv1[0:1]

        def body(r, carry):
            g, z, nbc = carry
            mx, first = _first_max(g, crow, _CAND_ROWS)
            g = jnp.where(crow == first, -jnp.inf, g)
            z = z + jnp.exp(mx - top)
            a_pick = jnp.where(first < 16, 0, jnp.where(first < 72, jnp.right_shift(first - 8, 3), first - 64))
            nbc = nbc + jnp.where(r16 == a_pick, 1.0, 0.0)
            return g, z, nbc

        _, z, nbc = lax.fori_loop(0, PEER_TOPK, body,
                                  (cand, jnp.zeros((1, tt), F32), jnp.zeros((PEER_TOPK, tt), F32)))
        lim0 = jnp.zeros((PEER_NKEYS, tt), F32)
        for a in range(PEER_TOPK):
            lim0 = jnp.where(rank0 == float(a), nbc[a:a + 1], lim0)
        rank1_ref[hh] = rank1
        lim0_ref[hh] = lim0
        e1_ref[hh] = jnp.exp(s[1] - sv1[0:1])
        e0_ref[hh] = jnp.exp(s[0] - sv0[0:1]) / z


def _peer_route(h2, wqt, subkeys, tt):
    n, d = h2.shape
    spec = pl.BlockSpec((PEER_HEADS, PEER_NKEYS, tt), lambda t: (0, 0, t))
    shp = jax.ShapeDtypeStruct((PEER_HEADS, PEER_NKEYS, n), F32)
    return pl.pallas_call(
        functools.partial(_peer_route_kernel, tt=tt),
        grid=(n // tt,),
        in_specs=[pl.BlockSpec((tt, d), lambda t: (t, 0)),
                  pl.BlockSpec(wqt.shape, lambda t: (0, 0)),
                  pl.BlockSpec(subkeys.shape, lambda t: (0, 0, 0))],
        out_specs=[spec] * 4,
        out_shape=[shp] * 4,
        compiler_params=_cparams(("arbitrary",)),
        name="peer_route",
    )(h2, wqt, subkeys)


def _gelu(x):
    return 0.5 * x * (1.0 + lax.erf(x * (2.0 ** -0.5)))


def _peer_expert_kernel(h_ref, u_ref, vt_ref, rank1_ref, e1_ref, lim0_ref, e0_ref, x1_ref, gate_ref,
                        gfin_ref, y_ref, acc_ref, gt_ref, *, tt, te):
    e = pl.program_id(1)

    @pl.when(e == 0)
    def _():
        acc_ref[...] = jnp.zeros_like(acc_ref)

    a_t = _nt_dot(u_ref[...], h_ref[...])
    for ii in range(te // PEER_NKEYS):
        i = e * (te // PEER_NKEYS) + ii
        w = jnp.zeros((PEER_NKEYS, tt), F32)
        for hh in range(PEER_HEADS):
            lim = lim0_ref[hh, pl.ds(i, 1), :]
            e0 = e0_ref[hh, pl.ds(i, 1), :]
            w = w + jnp.where(rank1_ref[hh] < lim, e1_ref[hh] * e0, 0.0)
        sl = slice(ii * PEER_NKEYS, (ii + 1) * PEER_NKEYS)
        gt_ref[sl, :] = (w * _gelu(a_t[sl])).astype(BF16)
    acc_ref[...] += jnp.dot(vt_ref[...], gt_ref[...], preferred_element_type=F32)

    @pl.when(e == pl.num_programs(1) - 1)
    def _():
        x2 = x1_ref[...] + gate_ref[...] * acc_ref[...].T
        y_ref[...] = x2 * _rms_scale(x2) * gfin_ref[...]


def _peer_experts(h2, u_bf, vt_bf, route, x1, gate, g_final, tt, te, tokens_per_gate):
    n, d = h2.shape
    ne = u_bf.shape[0]
    rspec = pl.BlockSpec((PEER_HEADS, PEER_NKEYS, tt), lambda t, e: (0, 0, t))
    tokspec = pl.BlockSpec((tt, d), lambda t, e: (t, 0))
    if gate.shape[1] == 1:
        gspec = pl.BlockSpec((None, 1, d), lambda t, e: (t * tt // tokens_per_gate, 0, 0))
    else:
        gspec = pl.BlockSpec((None, tt, d), lambda t, e: (0, t, 0))
    return pl.pallas_call(
        functools.partial(_peer_expert_kernel, tt=tt, te=te),
        grid=(n // tt, ne // te),
        in_specs=[tokspec,
                  pl.BlockSpec((te, d), lambda t, e: (e, 0)),
                  pl.BlockSpec((d, te), lambda t, e: (0, e)),
                  rspec, rspec, rspec, rspec, tokspec, gspec,
                  pl.BlockSpec((1, d), lambda t, e: (0, 0))],
        out_specs=tokspec,
        out_shape=jax.ShapeDtypeStruct((n, d), F32),
        scratch_shapes=[pltpu.VMEM((d, tt), F32), pltpu.VMEM((te, tt), BF16)],
        compiler_params=_cparams(("arbitrary", "arbitrary")),
        name="peer_experts",
    )(h2, u_bf, vt_bf, *route, x1, gate, g_final.reshape(1, d))


def _inproj_sample_kernel(x_ref, shift_ref, scale_ref, g_ref, w_ref, bf_ref, tab_ref,
                          qm_ref, km_ref, vm_ref, qf_ref, kf_ref, vf_ref, logf_ref):
    x = x_ref[...]
    h = (x * _rms_scale(x) * g_ref[...]) * (1.0 + scale_ref[...]) + shift_ref[...]
    p = jnp.dot(h.astype(BF16), w_ref[...], preferred_element_type=F32)
    w = GROUP_W
    for pp in range(w // LANES):
        sl = slice(pp * LANES, (pp + 1) * LANES)
        qm_ref[:, sl] = _rope_nat(p[:, sl], tab_ref)
        km_ref[:, sl] = _rope_nat(p[:, w + pp * LANES:w + (pp + 1) * LANES], tab_ref)
    vm_ref[...] = p[:, 2 * w:3 * w]
    qf_ref[...] = p[:, 3 * w:4 * w]
    kf_ref[...] = p[:, 4 * w:5 * w]
    vf_ref[...] = p[:, 5 * w:6 * w]
    lane = lax.broadcasted_iota(jnp.int32, logf_ref.shape, 1)
    logf_ref[...] = jnp.where(lane < N_GROUP_HEADS, _log_sigmoid(p[:, 6 * w:6 * w + LANES] + bf_ref[...]), 0.0)


def _inproj_sample(x, shift, scale, g_attn, w_in, b_forget, pos):
    n, d = x.shape
    nh = N_GROUP_HEADS
    wqm, wkm, wvm, wqf, wkf, wvf, wfl = _split_w_in(w_in)
    w = jnp.concatenate([wqm * Q_SCALE, wkm, wvm, wqf * Q_SCALE, wkf, wvf,
                         jnp.pad(wfl, ((0, 0), (0, LANES - nh)))], axis=1).astype(BF16)
    bf = jnp.pad(b_forget, (0, LANES - nh)).reshape(1, LANES)
    tab, _ = _rope_tables(pos)
    full = lambda shape: pl.BlockSpec(shape, lambda i: (0,) * len(shape))
    wide = jax.ShapeDtypeStruct((n, GROUP_W), F32)
    return pl.pallas_call(
        _inproj_sample_kernel,
        grid=(1,),
        in_specs=[full((n, d)), full((n, d)), full((n, d)), full((1, d)), full(w.shape),
                  full((1, LANES)), full(tab.shape)],
        out_specs=[full((n, GROUP_W))] * 6 + [full((n, LANES))],
        out_shape=[wide] * 6 + [jax.ShapeDtypeStruct((n, LANES), F32)],
        compiler_params=_cparams(("arbitrary",)),
        name="inproj_sample",
    )(x, shift, scale, g_attn.reshape(1, d), w, bf, tab)


def _expand_heads(a):
    n = a.shape[0]
    return jnp.concatenate([jnp.broadcast_to(a[hh:hh + 1], (SUBLANES, a.shape[1])) for hh in range(n)],
                           axis=0)


def _dot3r(x, b01):
    hi, mid, lo = _split3(x)
    d = functools.partial(jnp.dot, preferred_element_type=F32)
    return d(hi, b01) + d(mid, b01) + d(lo, b01)


def _online_update(s, v, m_ref, l_ref, acc_ref):
    m_old = m_ref[...]
    m_new = jnp.maximum(m_old, jnp.max(s, axis=1, keepdims=True))
    alpha = jnp.exp2(m_old - m_new)
    p = jnp.exp2(s - m_new)
    l_ref[...] = alpha * l_ref[...] + jnp.sum(p, axis=1, keepdims=True)
    acc_ref[...] = alpha * acc_ref[...] + jnp.dot(p.astype(BF16), v, preferred_element_type=F32)
    m_ref[...] = m_new


def _sample_attn_kernel(pt_ref, qm_ref, qf_ref, knm_ref, vnm_ref, knf_ref, vnf_ref, lnew_ref,
                        km_ref, vm_ref, kf_ref, vf_ref, lf_ref,
                        om_ref, of_ref,
                        mf_ref, lfs_ref, accf_ref, csuf_ref,
                        mb_ref, lb_ref, accb_ref, ksum_ref, gates_ref, mstat_ref, lstat_ref, ob_ref,
                        *, n_pages, rows, pages_per_block):
    del pt_ref
    j = pl.program_id(1)
    jj = n_pages - 1 - j
    nblk = n_pages // pages_per_block
    page = km_ref.shape[0]
    lane = lax.broadcasted_iota(jnp.int32, (rows, LANES), 1)
    trow = jnp.bitwise_and(lax.broadcasted_iota(jnp.int32, (rows, LANES), 0), SUBLANES - 1)

    @pl.when(j == 0)
    def _():
        mf_ref[...] = jnp.full_like(mf_ref, -jnp.inf)
        lfs_ref[...] = jnp.zeros_like(lfs_ref)
        accf_ref[...] = jnp.zeros_like(accf_ref)
        csuf_ref[...] = jnp.zeros_like(csuf_ref)
        gates_ref[...] = jnp.zeros_like(gates_ref)
        mstat_ref[...] = jnp.zeros_like(mstat_ref)
        lstat_ref[...] = jnp.zeros_like(lstat_ref)

    @pl.when(jj % pages_per_block == pages_per_block - 1)
    def _():
        mb_ref[...] = jnp.full_like(mb_ref, -jnp.inf)
        lb_ref[...] = jnp.zeros_like(lb_ref)
        accb_ref[...] = jnp.zeros_like(accb_ref)
        ksum_ref[...] = jnp.zeros_like(ksum_ref)

    qm = qm_ref[...]
    qf = qf_ref[...]
    qm_b = qm.astype(BF16)
    qf_b = qf.astype(BF16)

    a_new = jnp.where(lane <= trow, lnew_ref[...], 0.0)
    cq = jnp.sum(a_new, axis=1, keepdims=True)

    lexp = _expand_heads(lf_ref[...])
    kr = lax.broadcasted_iota(jnp.int32, (page, page), 0)
    kc = lax.broadcasted_iota(jnp.int32, (page, page), 1)
    tri_after = jnp.where(kr > kc, 1.0, 0.0).astype(BF16)
    decay = _dot3r(lexp, tri_after) + csuf_ref[...] + cq
    s_f = _nt_dot(qf_b, kf_ref[...].astype(BF16)) + decay * LOG2E
    _online_update(s_f, vf_ref[...].astype(BF16), mf_ref, lfs_ref, accf_ref)
    csuf_ref[...] += jnp.sum(lexp, axis=1, keepdims=True)

    k_m = km_ref[...]
    _online_update(_nt_dot(qm_b, k_m.astype(BF16)), vm_ref[...].astype(BF16), mb_ref, lb_ref, accb_ref)
    ksum_ref[...] += jnp.sum(k_m, axis=0, keepdims=True)

    @pl.when(jj % pages_per_block == 0)
    def _():
        n = jj // pages_per_block
        kmean = ksum_ref[...] * (1.0 / (page * pages_per_block))
        gate = jnp.sum(qm * kmean, axis=1, keepdims=True)
        hit = lane == n
        gates_ref[...] = jnp.where(hit, gate, gates_ref[...])
        mstat_ref[...] = jnp.where(hit, mb_ref[...], mstat_ref[...])
        lstat_ref[...] = jnp.where(hit, lb_ref[...], lstat_ref[...])
        ob_ref[n] = accb_ref[...]

    @pl.when(j == n_pages - 1)
    def _():
        causal = lane <= trow
        tri_new = jnp.where(kr > kc, 1.0, 0.0).astype(BF16)
        bias_new = _dot3r(a_new, tri_new)
        s_new = _nt_dot(qf_b, knf_ref[...].astype(BF16)) + bias_new * LOG2E
        _online_update(jnp.where(causal, s_new, -jnp.inf), vnf_ref[...].astype(BF16),
                       mf_ref, lfs_ref, accf_ref)
        of_ref[...] = accf_ref[...] / lfs_ref[...]

        s_own = jnp.where(causal, _nt_dot(qm_b, knm_ref[...].astype(BF16)), -jnp.inf)
        m_own = jnp.max(s_own, axis=1, keepdims=True)
        p_own = jnp.exp2(s_own - m_own)
        l_own = jnp.sum(p_own, axis=1, keepdims=True)
        o_own = jnp.dot(p_own.astype(BF16), vnm_ref[...].astype(BF16), preferred_element_type=F32)

        valid = lane < nblk
        g = jnp.where(valid, gates_ref[...], -jnp.inf)
        sel = lane < 0
        for _ in range(MOBA_TOPK):
            mx = jnp.max(g, axis=1, keepdims=True)
            first = jnp.min(jnp.where(g == mx, lane, LANES), axis=1, keepdims=True)
            pick = lane == first
            sel = sel | (pick & valid)
            g = jnp.where(pick, -jnp.inf, g)
        mstat = mstat_ref[...]
        m_all = jnp.maximum(m_own, jnp.max(jnp.where(sel, mstat, -jnp.inf), axis=1, keepdims=True))
        wgt = jnp.where(sel, jnp.exp2(mstat - m_all), 0.0)
        w_own = jnp.exp2(m_own - m_all)
        denom = jnp.sum(wgt * lstat_ref[...], axis=1, keepdims=True) + w_own * l_own

        def body(n, numer):
            wcol = jnp.sum(jnp.where(lane == n, wgt, 0.0), axis=1, keepdims=True)
            return numer + wcol * ob_ref[n]

        numer = lax.fori_loop(0, nblk, body, w_own * o_own)
        om_ref[...] = numer / denom


def _sample_attention(page_table, qbd_m, qbd_f, knew_m, vnew_m, knew_f, vnew_f, lnew,
                      ck_m, cv_m, ck_f, cv_f, clf_t):
    nseq, rows, w = qbd_m.shape
    n_pages = page_table.shape[1]
    page = ck_m.shape[1]
    pages_per_block = MOBA_BLOCK // page
    nblk = n_pages // pages_per_block
    pt_flat = page_table.reshape(-1)

    seq = lambda shape: pl.BlockSpec((None,) + shape, lambda b, j, pt: (b,) + (0,) * len(shape))
    pg = lambda shape: pl.BlockSpec(
        (None,) + shape, lambda b, j, pt: (pt[b * n_pages + n_pages - 1 - j],) + (0,) * len(shape))
    col = lambda: pltpu.VMEM((rows, 1), F32)
    grid_spec = pltpu.PrefetchScalarGridSpec(
        num_scalar_prefetch=1,
        grid=(nseq, n_pages),
        in_specs=[seq((rows, w)), seq((rows, w)), seq((page, w)), seq((page, w)), seq((page, w)),
                  seq((page, w)), seq((rows, LANES)),
                  pg((page, w)), pg((page, w)), pg((page, w)), pg((page, w)), pg((N_GROUP_HEADS, page))],
        out_specs=[seq((rows, w)), seq((rows, w))],
        scratch_shapes=[col(), col(), pltpu.VMEM((rows, w), F32), col(),
                        col(), col(), pltpu.VMEM((rows, w), F32), pltpu.VMEM((1, w), F32),
                        pltpu.VMEM((rows, LANES), F32), pltpu.VMEM((rows, LANES), F32),
                        pltpu.VMEM((rows, LANES), F32), pltpu.VMEM((nblk, rows, w), F32)])
    return pl.pallas_call(
        functools.partial(_sample_attn_kernel, n_pages=n_pages, rows=rows,
                          pages_per_block=pages_per_block),
        grid_spec=grid_spec,
        out_shape=[jax.ShapeDtypeStruct((nseq, rows, w), F32)] * 2,
        compiler_params=_cparams(("arbitrary", "arbitrary")),
        name="sample_attention",
    )(pt_flat, qbd_m, qbd_f, knew_m, vnew_m, knew_f, vnew_f, lnew, ck_m, cv_m, ck_f, cv_f, clf_t)


def _block_diag_queries(q, nseq, t):
    nh = N_GROUP_HEADS
    q4 = q.reshape(nseq, t, nh, HEAD_DIM)
    eye = jnp.eye(nh, dtype=q.dtype)
    return jnp.einsum("bthd,hg->bhtgd", q4, eye).reshape(nseq, nh * t, nh * HEAD_DIM)


def _diag_heads(o, nseq, t):
    nh = N_GROUP_HEADS
    o5 = o.reshape(nseq, nh, t, nh, HEAD_DIM)
    eye = jnp.eye(nh, dtype=o.dtype)
    return jnp.einsum("bhtgd,hg->bthd", o5, eye).reshape(nseq * t, nh * HEAD_DIM)


def _sample_mixers(x, mod, g_attn, w_in, b_forget, caches, page_table):
    nseq, t, d = x.shape
    nh = N_GROUP_HEADS
    n = nseq * t
    ck_m, cv_m, ck_f, cv_f, clf = caches
    n_pool, page = ck_m.shape[0], ck_m.shape[1]
    past_len = page_table.shape[1] * page
    pos = jnp.tile(past_len + jnp.arange(t, dtype=jnp.int32), nseq)
    per_tok = lambda a: jnp.broadcast_to(a[:, None, :], (nseq, t, d)).reshape(n, d)
    qm, km, vm, qf, kf, vf, logf = _inproj_sample(
        x.reshape(n, d), per_tok(mod[:, 0]), per_tok(mod[:, 1]), g_attn, w_in, b_forget, pos)
    pad_rows = lambda a: jnp.pad(a.reshape(nseq, t, GROUP_W), ((0, 0), (0, page - t), (0, 0)))
    lf_new = logf[:, 0:nh].reshape(nseq, t, nh)
    lnew = jnp.pad(jnp.repeat(jnp.swapaxes(lf_new, 1, 2), t, axis=1), ((0, 0), (0, 0), (0, LANES - t)))
    flat = lambda c: c.reshape(n_pool, page, GROUP_W)
    om, of = _sample_attention(
        page_table, _block_diag_queries(qm, nseq, t), _block_diag_queries(qf, nseq, t),
        pad_rows(km), pad_rows(vm), pad_rows(kf), pad_rows(vf), lnew,
        flat(ck_m), flat(cv_m), flat(ck_f), flat(cv_f), jnp.swapaxes(clf, 1, 2))
    otm = _diag_heads(om, nseq, t).T[None]
    otf = _diag_heads(of, nseq, t).T[None]
    shp = (nseq, t, nh, HEAD_DIM)
    new_rows = (km.reshape(shp), vm.reshape(shp), kf.reshape(shp), vf.reshape(shp), lf_new)
    return otm, otf, new_rows


def _prompt_mixers(x, mod, g_attn, w_in, b_forget):
    b, s, d = x.shape
    (km, vm, kf, vf, logf, kpm, kpf, qtm, qtg, qtf, fqt, vtm, vtf, kmean) = _inproj_prompt(
        x, mod[:, 0:1], mod[:, 1:2], g_attn, w_in, b_forget, ts=MOBA_BLOCK)
    bias = _moba_gate(qtg, kmean, tg=min(s, 2048))
    otm = _attention(qtm, bias, kpm, vtm, MOBA_EXT_ROWS, MOBA_BLOCK)
    otf = _attention(qtf, fqt, kpf, vtf, FOX_EXT_ROWS, MOBA_BLOCK)
    nh = N_GROUP_HEADS
    new_rows = (km.reshape(b, s, nh, HEAD_DIM), vm.reshape(b, s, nh, HEAD_DIM),
                kf.reshape(b, s, nh, HEAD_DIM), vf.reshape(b, s, nh, HEAD_DIM), logf[:, :, 0:nh])
    return otm, otf, new_rows


def _peer_ffn(h2, x1, gate, peer_tables, g_final, tt_route, tt, te, tokens_per_gate):
    wqt, subkeys, u_bf, vt_bf = peer_tables
    route = _peer_route(h2, wqt, subkeys, tt_route)
    return _peer_experts(h2, u_bf, vt_bf, route, x1, gate, g_final, tt, te, tokens_per_gate)


def kernel(x_prompt, x_sample, c_prompt, c_sample, cache_k_moba, cache_v_moba, cache_k_fox, cache_v_fox, cache_logf_fox, page_table, w_ada, b_ada, g_attn, g_ffn, w_in, b_forget, g_out_moba, g_out_fox, w_out, peer_wq, peer_subkeys, peer_u, peer_v, g_final):
    b, s, d = x_prompt.shape
    nseq, t, _ = x_sample.shape
    depth = w_ada.shape[0]
    assert depth == 1 and t == SUBLANES and s % MOBA_BLOCK == 0 and s // MOBA_BLOCK <= MOBA_EXT_ROWS
    l = 0

    c_all = jnp.concatenate([c_prompt, c_sample], axis=0)
    pad = -c_all.shape[0] % SUBLANES
    mod = _modulation(jnp.pad(c_all, ((0, pad), (0, 0))), w_ada[l], b_ada[l])
    mod = mod[:b + nseq].reshape(b + nseq, N_ADA, d)
    mp, ms = mod[:b], mod[b:]

    peer_tables = (peer_wq[l].T.astype(BF16),
                   peer_subkeys[l].reshape(PEER_HEADS * 2, PEER_NKEYS, PEER_HALF),
                   peer_u[l].astype(BF16), peer_v[l].T.astype(BF16))

    otm, otf, rows_p = _prompt_mixers(x_prompt, mp, g_attn[l], w_in[l], b_forget[l])
    x1, h2 = _merge(x_prompt, otm, otf, mp[:, 2:3], mp[:, 3:4], mp[:, 4:5],
                    g_out_moba[l], g_out_fox[l], g_ffn[l], w_out[l], tm=min(s, 512))
    y_prompt = _peer_ffn(h2.reshape(b * s, d), x1.reshape(b * s, d), mp[:, 5:6], peer_tables, g_final,
                         tt_route=256, tt=min(s, 512), te=1024, tokens_per_gate=s).reshape(b, s, d)

    n = nseq * t
    per_tok = lambda a: jnp.broadcast_to(a[:, None, :], (nseq, t, d)).reshape(1, n, d)
    caches = (cache_k_moba[l], cache_v_moba[l], cache_k_fox[l], cache_v_fox[l], cache_logf_fox[l])
    otm_s, otf_s, rows_s = _sample_mixers(x_sample, ms, g_attn[l], w_in[l], b_forget[l], caches, page_table)
    x1s, h2s = _merge(x_sample.reshape(1, n, d), otm_s, otf_s, per_tok(ms[:, 2]), per_tok(ms[:, 3]),
                      per_tok(ms[:, 4]), g_out_moba[l], g_out_fox[l], g_ffn[l], w_out[l], tm=n)
    y_sample = _peer_ffn(h2s.reshape(n, d), x1s.reshape(n, d), per_tok(ms[:, 5]), peer_tables, g_final,
                         tt_route=n, tt=n, te=1024, tokens_per_gate=n).reshape(nseq, t, d)

    stack = lambda a: a[None]
    return (y_prompt, y_sample) + tuple(stack(a) for a in rows_p) + tuple(stack(a) for a in rows_s)
```

```python
import functools
import math

import numpy as np
import jax
import jax.numpy as jnp
from jax import lax
from jax.experimental import pallas as pl
from jax.experimental.pallas import tpu as pltpu

F32 = jnp.float32
BF16 = jnp.bfloat16
HIGHEST = lax.Precision.HIGHEST

HEAD_DIM = 64
N_GROUP_HEADS = 8
GROUP_W = N_GROUP_HEADS * HEAD_DIM
MOBA_BLOCK = 256
MOBA_BLOCK_LOG2 = 8
MOBA_TOPK = 3
ROPE_THETA = 500000.0
ROPE_DIM = HEAD_DIM // 4
ROPE_HALF = ROPE_DIM // 2
PEER_HEADS = 8
PEER_NKEYS = 128
PEER_HALF = 128
PEER_TOPK = 16
N_ADA = 6
EPS = 1e-6

LANES = 128
SUBLANES = 8
VMEM_LIMIT_BYTES = 56 * 1024 * 1024

LOG2E = math.log2(math.e)
Q_SCALE = HEAD_DIM ** -0.5 * LOG2E
NEG_BIG = -1.0e30
PAD_W = 2 * HEAD_DIM
FOX_EXT_ROWS = 16
MOBA_EXT_ROWS = 32


def _cparams(sem):
    return pltpu.CompilerParams(dimension_semantics=sem, vmem_limit_bytes=VMEM_LIMIT_BYTES)


def _nt_dot(a, b):
    return lax.dot_general(a, b, (((1,), (1,)), ((), ())), preferred_element_type=F32)


def _split3(x):
    hi = x.astype(BF16)
    r1 = x - hi.astype(F32)
    mid = r1.astype(BF16)
    lo = (r1 - mid.astype(F32)).astype(BF16)
    return hi, mid, lo


def _dot3(a01, x):
    hi, mid, lo = _split3(x)
    d = functools.partial(jnp.dot, preferred_element_type=F32)
    return d(a01, hi) + d(a01, mid) + d(a01, lo)


def _log_sigmoid(z):
    return jnp.minimum(z, 0.0) - jnp.log1p(jnp.exp(-jnp.abs(z)))


def _rms_scale(x):
    return lax.rsqrt(jnp.mean(x * x, axis=-1, keepdims=True) + EPS)


def _mod_kernel(c_ref, w_ref, b_ref, o_ref):
    c = c_ref[...]
    s = c / (1.0 + jnp.exp(-c))
    o_ref[...] = jnp.dot(s, w_ref[...], precision=HIGHEST, preferred_element_type=F32) + b_ref[...]


def _modulation(c_all, w_ada, b_ada):
    m, d = c_all.shape
    n = w_ada.shape[1]
    tn = 1536
    return pl.pallas_call(
        _mod_kernel,
        grid=(n // tn,),
        in_specs=[pl.BlockSpec((m, d), lambda j: (0, 0)),
                  pl.BlockSpec((d, tn), lambda j: (0, j)),
                  pl.BlockSpec((1, tn), lambda j: (0, j))],
        out_specs=pl.BlockSpec((m, tn), lambda j: (0, j)),
        out_shape=jax.ShapeDtypeStruct((m, n), F32),
        compiler_params=_cparams(("arbitrary",)),
        name="modulation",
    )(c_all, w_ada, b_ada.reshape(1, n))


def _rope_tables(pos):
    inv = ROPE_THETA ** (-jnp.arange(ROPE_HALF, dtype=F32) / ROPE_HALF)
    ang = pos.astype(F32)[:, None] * inv[None, :]
    cos, sin = jnp.cos(ang), jnp.sin(ang)
    t = pos.shape[0]
    one = jnp.ones((t, HEAD_DIM - ROPE_DIM), F32)
    zero = jnp.zeros((t, HEAD_DIM - ROPE_DIM), F32)
    z8 = jnp.zeros((t, ROPE_HALF), F32)
    c = jnp.concatenate([cos, cos, one], axis=1)
    sm = jnp.concatenate([-sin, z8, zero], axis=1)
    sp = jnp.concatenate([z8, sin, zero], axis=1)
    nat = jnp.stack([jnp.tile(a, (1, 2)) for a in (c, sm, sp)])
    tr = jnp.stack([cos.T, sin.T])
    return nat, tr


def _rope_nat(x, tab_ref):
    xm = pltpu.roll(x, LANES - ROPE_HALF, axis=1)
    xp = pltpu.roll(x, ROPE_HALF, axis=1)
    return x * tab_ref[0] + xm * tab_ref[1] + xp * tab_ref[2]


def _inproj_kernel(x_ref, shift_ref, scale_ref, g_ref, wnat_ref, wt_ref, bf_ref, tab_ref, tabt_ref,
                   ltri_ref, pk_ref, pq_ref,
                   km_ref, vm_ref, kf_ref, vf_ref, logf_ref, kpm_ref, kpf_ref,
                   qtm_ref, qtg_ref, qtf_ref, fqt_ref, vtm_ref, vtf_ref, kmean_ref,
                   carry_ref, *, ts):
    j = pl.program_id(1)
    nh = N_GROUP_HEADS
    x = x_ref[...]
    h = (x * _rms_scale(x) * g_ref[...]) * (1.0 + scale_ref[...]) + shift_ref[...]
    hb = h.astype(BF16)
    pn = jnp.dot(hb, wnat_ref[...], preferred_element_type=F32)
    pt = _nt_dot(wt_ref[...], hb)
    w_sp = nh * PAD_W
    kpm = pn[:, 0:w_sp]
    kpf = pn[:, w_sp:2 * w_sp]
    vm_ref[...] = pn[:, 2 * w_sp:2 * w_sp + GROUP_W]
    vf_ref[...] = pn[:, 2 * w_sp + GROUP_W:2 * w_sp + 2 * GROUP_W]
    fl = pn[:, 2 * w_sp + 2 * GROUP_W:2 * w_sp + 2 * GROUP_W + LANES]

    lane = lax.broadcasted_iota(jnp.int32, (ts, PAD_W), 1)
    row = lax.broadcasted_iota(jnp.int32, (ts, PAD_W), 0)
    blk = jnp.right_shift(j * ts + row, MOBA_BLOCK_LOG2)
    onehot = lane == HEAD_DIM + blk
    km_heads = [_rope_nat(kpm[:, hh * PAD_W:(hh + 1) * PAD_W], tab_ref) for hh in range(nh)]
    nblk = ts // MOBA_BLOCK
    for hh in range(nh):
        kh = km_heads[hh]
        kpm_ref[:, hh * PAD_W:(hh + 1) * PAD_W] = jnp.where(onehot, 1.0, kh).astype(BF16)
        for bb in range(nblk):
            kmean_ref[bb, :, hh * PAD_W:(hh + 1) * PAD_W] = jnp.mean(
                kh[bb * MOBA_BLOCK:(bb + 1) * MOBA_BLOCK], axis=0, keepdims=True)
    for pp in range(nh // 2):
        km_ref[:, pp * PAD_W:(pp + 1) * PAD_W] = (
            km_heads[2 * pp] + pltpu.roll(km_heads[2 * pp + 1], HEAD_DIM, axis=1))
        kf_ref[:, pp * PAD_W:(pp + 1) * PAD_W] = (
            kpf[:, 2 * pp * PAD_W:(2 * pp + 1) * PAD_W]
            + pltpu.roll(kpf[:, (2 * pp + 1) * PAD_W:(2 * pp + 2) * PAD_W], HEAD_DIM, axis=1))

    lane1 = lax.broadcasted_iota(jnp.int32, (ts, LANES), 1)
    logf = jnp.where(lane1 < nh, _log_sigmoid(fl + bf_ref[...]), 0.0)
    logf_ref[...] = logf

    @pl.when(j == 0)
    def _():
        carry_ref[...] = jnp.zeros_like(carry_ref)

    cum = _dot3(ltri_ref[...], logf) + carry_ref[0:1, :]
    carry_ref[0:1, :] = cum[ts - 1:ts, :]
    g = cum * LOG2E
    g_hi, g_mid, g_lo = _split3(g)
    parts = (g_hi.astype(F32) + pltpu.roll(g_mid.astype(F32), nh, axis=1)
             + pltpu.roll(g_lo.astype(F32), 2 * nh, axis=1)
             + jnp.where(lane1 == 3 * nh, 1.0, 0.0)).astype(BF16)
    kpf_ref[...] = (kpf + jnp.dot(parts, pk_ref[...], preferred_element_type=F32)).astype(BF16)
    fqt_ref[...] = _nt_dot(pq_ref[...], parts).astype(BF16)

    qtm = pt[0:GROUP_W]
    cos_t = tabt_ref[0]
    sin_t = tabt_ref[1]
    pieces = []
    for hh in range(nh):
        base = hh * HEAD_DIM
        x1 = qtm[base:base + ROPE_HALF]
        x2 = qtm[base + ROPE_HALF:base + ROPE_DIM]
        pieces += [x1 * cos_t - x2 * sin_t, x2 * cos_t + x1 * sin_t, qtm[base + ROPE_DIM:base + HEAD_DIM]]
    qtm = jnp.concatenate(pieces, axis=0)
    qtg_ref[...] = qtm
    qtm_ref[...] = qtm.astype(BF16)
    qtf_ref[...] = pt[GROUP_W:2 * GROUP_W].astype(BF16)
    vtm = pt[2 * GROUP_W:3 * GROUP_W].astype(BF16)
    vtf = pt[3 * GROUP_W:4 * GROUP_W].astype(BF16)
    for bb in range(nblk):
        sl = slice(bb * MOBA_BLOCK, (bb + 1) * MOBA_BLOCK)
        for hh in range(nh):
            vtm_ref[hh, bb] = vtm[hh * HEAD_DIM:(hh + 1) * HEAD_DIM, sl]
            vtf_ref[hh, bb] = vtf[hh * HEAD_DIM:(hh + 1) * HEAD_DIM, sl]


def _spread_heads(w):
    d = w.shape[0]
    w = w.reshape(d, N_GROUP_HEADS, HEAD_DIM)
    return jnp.pad(w, ((0, 0), (0, 0), (0, PAD_W - HEAD_DIM))).reshape(d, N_GROUP_HEADS * PAD_W)


def _split_w_in(w_in):
    cuts = np.cumsum([GROUP_W] * 6).tolist()
    return jnp.split(w_in, cuts, axis=1)


def _fox_placement():
    nh = N_GROUP_HEADS
    pk = np.zeros((LANES, nh * PAD_W), np.float32)
    pq = np.zeros((nh * FOX_EXT_ROWS, LANES), np.float32)
    for hh in range(nh):
        for t in range(3):
            pk[t * nh + hh, hh * PAD_W + HEAD_DIM + t] = -1.0
            pk[3 * nh, hh * PAD_W + HEAD_DIM + 3 + t] = 1.0
            pq[hh * FOX_EXT_ROWS + t, 3 * nh] = 1.0
            pq[hh * FOX_EXT_ROWS + 3 + t, t * nh + hh] = 1.0
    return jnp.asarray(pk, BF16), jnp.asarray(pq, BF16)


def _inproj_prompt(x, shift, scale, g_attn, w_in, b_forget, ts):
    b, s, d = x.shape
    nh = N_GROUP_HEADS
    nb = s // MOBA_BLOCK
    wqm, wkm, wvm, wqf, wkf, wvf, wfl = _split_w_in(w_in)
    wnat = jnp.concatenate([_spread_heads(wkm), _spread_heads(wkf), wvm, wvf,
                            jnp.pad(wfl, ((0, 0), (0, LANES - nh)))], axis=1).astype(BF16)
    wt = jnp.concatenate([wqm * Q_SCALE, wqf * Q_SCALE, wvm, wvf], axis=1).T.astype(BF16)
    bf = jnp.pad(b_forget, (0, LANES - nh)).reshape(1, LANES)
    tab, tabt = _rope_tables(jnp.arange(s, dtype=jnp.int32))
    ltri = jnp.tril(jnp.ones((ts, ts), F32)).astype(BF16)
    pk, pq = _fox_placement()
    nblk = ts // MOBA_BLOCK
    w_sp = nh * PAD_W

    def tok(width, dtype):
        return (jax.ShapeDtypeStruct((b, s, width), dtype),
                pl.BlockSpec((None, ts, width), lambda i, j: (i, j, 0)))

    def tr(rows, dtype):
        return (jax.ShapeDtypeStruct((b, rows, s), dtype),
                pl.BlockSpec((None, rows, ts), lambda i, j: (i, 0, j)))

    vt = (jax.ShapeDtypeStruct((b, nh, nb, HEAD_DIM, MOBA_BLOCK), BF16),
          pl.BlockSpec((None, nh, nblk, HEAD_DIM, MOBA_BLOCK), lambda i, j: (i, 0, j, 0, 0)))
    kmean = (jax.ShapeDtypeStruct((b, nb, 1, w_sp), F32),
             pl.BlockSpec((None, nblk, 1, w_sp), lambda i, j: (i, j, 0, 0)))
    outs = [tok(GROUP_W, F32), tok(GROUP_W, F32), tok(GROUP_W, F32), tok(GROUP_W, F32),
            tok(LANES, F32), tok(w_sp, BF16), tok(w_sp, BF16),
            tr(GROUP_W, BF16), tr(GROUP_W, F32), tr(GROUP_W, BF16), tr(nh * FOX_EXT_ROWS, BF16),
            vt, vt, kmean]
    const = lambda shape: pl.BlockSpec(shape, lambda i, j: (0,) * len(shape))
    row_spec = pl.BlockSpec((None, 1, d), lambda i, j: (i, 0, 0))
    return pl.pallas_call(
        functools.partial(_inproj_kernel, ts=ts),
        grid=(b, s // ts),
        in_specs=[pl.BlockSpec((None, ts, d), lambda i, j: (i, j, 0)), row_spec, row_spec,
                  const((1, d)), const(wnat.shape), const(wt.shape), const((1, LANES)),
                  pl.BlockSpec((3, ts, LANES), lambda i, j: (0, j, 0)),
                  pl.BlockSpec((2, ROPE_HALF, ts), lambda i, j: (0, 0, j)),
                  const((ts, ts)), const(pk.shape), const(pq.shape)],
        out_specs=[o[1] for o in outs],
        out_shape=[o[0] for o in outs],
        scratch_shapes=[pltpu.VMEM((SUBLANES, LANES), F32)],
        compiler_params=_cparams(("arbitrary", "arbitrary")),
        name="inproj_prompt",
    )(x, shift, scale, g_attn.reshape(1, d), wnat, wt, bf, tab, tabt, ltri, pk, pq)


def _first_max(g, idx, big):
    mx = jnp.max(g, axis=0, keepdims=True)
    first = jnp.min(jnp.where(g == mx, idx, big), axis=0, keepdims=True)
    return mx, first


def _moba_gate_kernel(qt_ref, kmean_ref, bias_ref, *, tg, nb):
    i = pl.program_id(2)
    gate = jnp.dot(kmean_ref[...][:, 0:HEAD_DIM], qt_ref[...], precision=HIGHEST,
                   preferred_element_type=F32)
    n = lax.broadcasted_iota(jnp.int32, (nb, tg), 0)
    qblk = jnp.right_shift(i * tg + lax.broadcasted_iota(jnp.int32, (nb, tg), 1), MOBA_BLOCK_LOG2)
    past = n < qblk
    g = jnp.where(past, gate, -jnp.inf)
    sel = n == qblk
    for _ in range(MOBA_TOPK):
        _, first = _first_max(g, n, nb)
        pick = n == first
        sel = sel | (pick & past)
        g = jnp.where(pick, -jnp.inf, g)
    bias_ref[...] = jnp.where(sel, 0.0, NEG_BIG).astype(BF16)


def _moba_gate(qtg, kmean, tg):
    b, _, s = qtg.shape
    nb = MOBA_EXT_ROWS
    kmean = kmean.reshape(b, kmean.shape[1], N_GROUP_HEADS * PAD_W)
    kmean = jnp.pad(kmean, ((0, 0), (0, nb - kmean.shape[1]), (0, 0)))
    return pl.pallas_call(
        functools.partial(_moba_gate_kernel, tg=tg, nb=nb),
        grid=(b, N_GROUP_HEADS, s // tg),
        in_specs=[pl.BlockSpec((None, HEAD_DIM, tg), lambda bi, h, i: (bi, h, i)),
                  pl.BlockSpec((None, nb, PAD_W), lambda bi, h, i: (bi, 0, h))],
        out_specs=pl.BlockSpec((None, None, nb, tg), lambda bi, h, i: (bi, h, 0, i)),
        out_shape=jax.ShapeDtypeStruct((b, N_GROUP_HEADS, nb, s), BF16),
        compiler_params=_cparams(("arbitrary", "arbitrary", "arbitrary")),
        name="moba_gate",
    )(qtg, kmean)


ATTN_HEADS_PER_STEP = 4


def _attn_kernel(qt_ref, ext_ref, kp_ref, vt_ref, o_ref, qp_ref, *, tq, ext_rows, hb):
    i = pl.program_id(2)
    ext = ext_ref[...].reshape(hb * ext_rows, tq)
    for hh in range(hb):
        qp_ref[hh, 0:HEAD_DIM, :] = qt_ref[hh * HEAD_DIM:(hh + 1) * HEAD_DIM, :]
        qp_ref[hh, HEAD_DIM:HEAD_DIM + ext_rows, :] = ext[hh * ext_rows:(hh + 1) * ext_rows]
        qp_ref[hh, HEAD_DIM + ext_rows:PAD_W, :] = jnp.zeros((PAD_W - HEAD_DIM - ext_rows, tq), BF16)
    qps = [qp_ref[hh] for hh in range(hb)]

    def step(tiles, carry):
        s_all = [[jnp.dot(k[:, hh * PAD_W:(hh + 1) * PAD_W], qps[hh], preferred_element_type=F32)
                  for hh in range(hb)] for k, _, _ in tiles]
        out = []
        for hh in range(hb):
            m, l, acc = carry[3 * hh:3 * hh + 3]
            s_h = [s_all[t][hh] if mask is None else jnp.where(mask, s_all[t][hh], -jnp.inf)
                   for t, (_, _, mask) in enumerate(tiles)]
            m_new = m
            for s_t in s_h:
                m_new = jnp.maximum(m_new, jnp.max(s_t, axis=0, keepdims=True))
            alpha = jnp.exp2(m - m_new)
            l = alpha * l
            acc = alpha * acc
            for s_t, (_, vidx, _) in zip(s_h, tiles):
                p = jnp.exp2(s_t - m_new)
                l = l + jnp.sum(p, axis=0, keepdims=True)
                acc = acc + jnp.dot(vt_ref[hh, vidx], p.astype(BF16), preferred_element_type=F32)
            out += [m_new, l, acc]
        return tuple(out)

    def keys(n):
        return kp_ref[pl.ds(pl.multiple_of(n * tq, tq), tq), :]

    kpos = lax.broadcasted_iota(jnp.int32, (tq, tq), 0)
    qpos = lax.broadcasted_iota(jnp.int32, (tq, tq), 1)
    odd = i % 2
    init = []
    for hh in range(hb):
        init += [jnp.full((1, tq), -jnp.inf, F32), jnp.zeros((1, tq), F32), jnp.zeros((HEAD_DIM, tq), F32)]
    carry = step([(keys(i), i, kpos <= qpos), (keys(0), 0, kpos < odd * tq)], tuple(init))

    def body(u, carry):
        n = odd + 2 * u
        return step([(keys(n), n, None), (keys(n + 1), n + 1, None)], carry)

    fin = lax.fori_loop(0, i // 2, body, carry)
    for hh in range(hb):
        o_ref[hh * HEAD_DIM:(hh + 1) * HEAD_DIM, :] = fin[3 * hh + 2] / fin[3 * hh + 1]


def _attention(qt, ext, kp, vt, ext_rows, tq):
    b, _, s = qt.shape
    hb = ATTN_HEADS_PER_STEP
    ng = N_GROUP_HEADS // hb
    nb = s // tq
    if ext.ndim == 4:
        ext_spec = pl.BlockSpec((None, hb, ext_rows, tq), lambda bi, g, i: (bi, g, 0, i))
    else:
        ext_spec = pl.BlockSpec((None, hb * ext_rows, tq), lambda bi, g, i: (bi, g, i))
    return pl.pallas_call(
        functools.partial(_attn_kernel, tq=tq, ext_rows=ext_rows, hb=hb),
        grid=(b, ng, nb),
        in_specs=[pl.BlockSpec((None, hb * HEAD_DIM, tq), lambda bi, g, i: (bi, g, i)),
                  ext_spec,
                  pl.BlockSpec((None, s, hb * PAD_W), lambda bi, g, i: (bi, 0, g)),
                  pl.BlockSpec((None, hb, nb, HEAD_DIM, tq), lambda bi, g, i: (bi, g, 0, 0, 0))],
        out_specs=pl.BlockSpec((None, hb * HEAD_DIM, tq), lambda bi, g, i: (bi, g, i)),
        out_shape=jax.ShapeDtypeStruct((b, GROUP_W, s), F32),
        scratch_shapes=[pltpu.VMEM((hb, PAD_W, tq), BF16)],
        compiler_params=_cparams(("arbitrary", "arbitrary", "arbitrary")),
        name="attention",
    )(qt, ext, kp, vt)


def _merge_kernel(x_ref, otm_ref, otf_ref, gate_ref, shift_ref, scale_ref, gm_ref, gf_ref, gffn_ref,
                  wout_ref, x1_ref, h2_ref):
    def group_norm(ot, gcol):
        r = lax.rsqrt(jnp.mean(ot * ot, axis=0, keepdims=True) + EPS)
        return (ot * r * gcol).T

    on = jnp.concatenate([group_norm(otm_ref[...], gm_ref[...]),
                          group_norm(otf_ref[...], gf_ref[...])], axis=1).astype(BF16)
    y = jnp.dot(on, wout_ref[...], preferred_element_type=F32)
    x1 = x_ref[...] + gate_ref[...] * y
    x1_ref[...] = x1
    h2 = (x1 * _rms_scale(x1) * gffn_ref[...]) * (1.0 + scale_ref[...]) + shift_ref[...]
    h2_ref[...] = h2.astype(BF16)


def _mod_spec(arr, tm, d):
    if arr.shape[1] == 1:
        return pl.BlockSpec((None, 1, d), lambda i, j: (i, 0, 0))
    return pl.BlockSpec((None, tm, d), lambda i, j: (i, j, 0))


def _merge(x, otm, otf, gate, shift, scale, g_m, g_f, g_ffn, w_out, tm):
    b, s, d = x.shape
    tokspec = pl.BlockSpec((None, tm, d), lambda i, j: (i, j, 0))
    otspec = pl.BlockSpec((None, GROUP_W, tm), lambda i, j: (i, 0, j))
    const = lambda shape: pl.BlockSpec(shape, lambda i, j: (0,) * len(shape))
    return pl.pallas_call(
        _merge_kernel,
        grid=(b, s // tm),
        in_specs=[tokspec, otspec, otspec, _mod_spec(gate, tm, d), _mod_spec(shift, tm, d),
                  _mod_spec(scale, tm, d), const((GROUP_W, 1)), const((GROUP_W, 1)), const((1, d)),
                  const((2 * GROUP_W, d))],
        out_specs=[tokspec, tokspec],
        out_shape=[jax.ShapeDtypeStruct((b, s, d), F32), jax.ShapeDtypeStruct((b, s, d), BF16)],
        compiler_params=_cparams(("arbitrary", "arbitrary")),
        name="merge_outproj",
    )(x, otm, otf, gate, shift, scale, g_m.reshape(GROUP_W, 1), g_f.reshape(GROUP_W, 1),
      g_ffn.reshape(1, d), w_out.astype(BF16))


_CAND_GROUPS = [(0, 16), (1, 8), (2, 5), (3, 4), (4, 3), (5, 2), (6, 2), (7, 2)]
_CAND_ROWS = 16 + 7 * 8 + 8


def _top16(s, tt):
    idx = lax.broadcasted_iota(jnp.int32, (PEER_NKEYS, tt), 0)
    r16 = lax.broadcasted_iota(jnp.int32, (PEER_TOPK, tt), 0)

    def body(r, carry):
        g, rank, sv = carry
        mx, first = _first_max(g, idx, PEER_NKEYS)
        pick = idx == first
        rank = jnp.where(pick, r.astype(F32), rank)
        g = jnp.where(pick, -jnp.inf, g)
        sv = jnp.where(r16 == r, mx, sv)
        return g, rank, sv

    _, rank, sv = lax.fori_loop(
        0, PEER_TOPK, body,
        (s, jnp.full((PEER_NKEYS, tt), float(PEER_TOPK), F32), jnp.zeros((PEER_TOPK, tt), F32)))
    return sv, rank


def _peer_route_kernel(h_ref, wqt_ref, sub_ref, rank1_ref, e1_ref, lim0_ref, e0_ref, *, tt):
    qt = _nt_dot(wqt_ref[...], h_ref[...])
    r16 = lax.broadcasted_iota(jnp.int32, (PEER_TOPK, tt), 0)
    crow = lax.broadcasted_iota(jnp.int32, (_CAND_ROWS, tt), 0)
    for hh in range(PEER_HEADS):
        s = []
        for p in range(2):
            base = (hh * 2 + p) * PEER_HALF
            s.append(jnp.dot(sub_ref[hh * 2 + p], qt[base:base + PEER_HALF], precision=HIGHEST,
                             preferred_element_type=F32))
        sv0, rank0 = _top16(s[0], tt)
        sv1, rank1 = _top16(s[1], tt)

        groups = []
        for a, cnt in _CAND_GROUPS:
            rows = 16 if a == 0 else SUBLANES
            blk = sv0[a:a + 1] + sv1[0:rows]
            if cnt < rows:
                blk = jnp.where(lax.broadcasted_iota(jnp.int32, (rows, tt), 0) < cnt, blk, -jnp.inf)
            groups.append(blk)
        groups.append(sv0[8:16] + sv1[0:1])
        cand = jnp.concatenate(groups, axis=0)
        top = sv0[0:1] + sv1[0:1]

        def body(r, carry):
            g, z, nbc = carry
            mx, first = _first_max(g, crow, _CAND_ROWS)
            g = jnp.where(crow == first, -jnp.inf, g)
            z = z + jnp.exp(mx - top)
            a_pick = jnp.where(first < 16, 0, jnp.where(first < 72, jnp.right_shift(first - 8, 3), first - 64))
            nbc = nbc + jnp.where(r16 == a_pick, 1.0, 0.0)
            return g, z, nbc

        _, z, nbc = lax.fori_loop(0, PEER_TOPK, body,
                                  (cand, jnp.zeros((1, tt), F32), jnp.zeros((PEER_TOPK, tt), F32)))
        lim0 = jnp.zeros((PEER_NKEYS, tt), F32)
        for a in range(PEER_TOPK):
            lim0 = jnp.where(rank0 == float(a), nbc[a:a + 1], lim0)
        rank1_ref[hh] = rank1.astype(BF16)
        lim0_ref[hh] = lim0
        e1_ref[hh] = jnp.exp(s[1] - sv1[0:1]).astype(BF16)
        e0_ref[hh] = jnp.exp(s[0] - sv0[0:1]) / z


def _peer_route(h2, wqt, subkeys, tt):
    n, d = h2.shape
    spec = pl.BlockSpec((PEER_HEADS, PEER_NKEYS, tt), lambda t: (0, 0, t))
    shp = lambda dt: jax.ShapeDtypeStruct((PEER_HEADS, PEER_NKEYS, n), dt)
    return pl.pallas_call(
        functools.partial(_peer_route_kernel, tt=tt),
        grid=(n // tt,),
        in_specs=[pl.BlockSpec((tt, d), lambda t: (t, 0)),
                  pl.BlockSpec(wqt.shape, lambda t: (0, 0)),
                  pl.BlockSpec(subkeys.shape, lambda t: (0, 0, 0))],
        out_specs=[spec] * 4,
        out_shape=[shp(BF16), shp(BF16), shp(F32), shp(F32)],
        compiler_params=_cparams(("arbitrary",)),
        name="peer_route",
    )(h2, wqt, subkeys)


def _gelu(x):
    return 0.5 * x * (1.0 + lax.erf(x * (2.0 ** -0.5)))


def _peer_expert_kernel(h_ref, u_ref, vt_ref, rank1_ref, e1_ref, lim0_ref, e0_ref, x1_ref, gate_ref,
                        gfin_ref, y_ref, acc_ref, gt_ref, *, tt, te):
    e = pl.program_id(1)

    @pl.when(e == 0)
    def _():
        acc_ref[...] = jnp.zeros_like(acc_ref)

    a_t = _nt_dot(u_ref[...], h_ref[...])
    for ii in range(te // PEER_NKEYS):
        i = e * (te // PEER_NKEYS) + ii
        w = jnp.zeros((PEER_NKEYS, tt), BF16)
        for hh in range(PEER_HEADS):
            lim = lim0_ref[hh, pl.ds(i, 1), :].astype(BF16)
            e0 = e0_ref[hh, pl.ds(i, 1), :].astype(BF16)
            w = w + jnp.where(rank1_ref[hh] < lim, e1_ref[hh] * e0, jnp.zeros((), BF16))
        sl = slice(ii * PEER_NKEYS, (ii + 1) * PEER_NKEYS)
        gt_ref[sl, :] = w * _gelu(a_t[sl]).astype(BF16)
    acc_ref[...] += jnp.dot(vt_ref[...], gt_ref[...], preferred_element_type=F32)

    @pl.when(e == pl.num_programs(1) - 1)
    def _():
        x2 = x1_ref[...] + gate_ref[...] * acc_ref[...].T
        y_ref[...] = x2 * _rms_scale(x2) * gfin_ref[...]


def _peer_experts(h2, u_bf, vt_bf, route, x1, gate, g_final, tt, te, tokens_per_gate):
    n, d = h2.shape
    ne = u_bf.shape[0]
    rspec = pl.BlockSpec((PEER_HEADS, PEER_NKEYS, tt), lambda t, e: (0, 0, t))
    tokspec = pl.BlockSpec((tt, d), lambda t, e: (t, 0))
    if gate.shape[1] == 1:
        gspec = pl.BlockSpec((None, 1, d), lambda t, e: (t * tt // tokens_per_gate, 0, 0))
    else:
        gspec = pl.BlockSpec((None, tt, d), lambda t, e: (0, t, 0))
    return pl.pallas_call(
        functools.partial(_peer_expert_kernel, tt=tt, te=te),
        grid=(n // tt, ne // te),
        in_specs=[tokspec,
                  pl.BlockSpec((te, d), lambda t, e: (e, 0)),
                  pl.BlockSpec((d, te), lambda t, e: (0, e)),
                  rspec, rspec, rspec, rspec, tokspec, gspec,
                  pl.BlockSpec((1, d), lambda t, e: (0, 0))],
        out_specs=tokspec,
        out_shape=jax.ShapeDtypeStruct((n, d), F32),
        scratch_shapes=[pltpu.VMEM((d, tt), F32), pltpu.VMEM((te, tt), BF16)],
        compiler_params=_cparams(("arbitrary", "arbitrary")),
        name="peer_experts",
    )(h2, u_bf, vt_bf, *route, x1, gate, g_final.reshape(1, d))


def _inproj_sample_kernel(x_ref, shift_ref, scale_ref, g_ref, w_ref, bf_ref, tab_ref,
                          qm_ref, km_ref, vm_ref, qf_ref, kf_ref, vf_ref, logf_ref):
    x = x_ref[...]
    h = (x * _rms_scale(x) * g_ref[...]) * (1.0 + scale_ref[...]) + shift_ref[...]
    p = jnp.dot(h.astype(BF16), w_ref[...], preferred_element_type=F32)
    w = GROUP_W
    for pp in range(w // LANES):
        sl = slice(pp * LANES, (pp + 1) * LANES)
        qm_ref[:, sl] = _rope_nat(p[:, sl], tab_ref)
        km_ref[:, sl] = _rope_nat(p[:, w + pp * LANES:w + (pp + 1) * LANES], tab_ref)
    vm_ref[...] = p[:, 2 * w:3 * w]
    qf_ref[...] = p[:, 3 * w:4 * w]
    kf_ref[...] = p[:, 4 * w:5 * w]
    vf_ref[...] = p[:, 5 * w:6 * w]
    lane = lax.broadcasted_iota(jnp.int32, logf_ref.shape, 1)
    logf_ref[...] = jnp.where(lane < N_GROUP_HEADS, _log_sigmoid(p[:, 6 * w:6 * w + LANES] + bf_ref[...]), 0.0)


def _inproj_sample(x, shift, scale, g_attn, w_in, b_forget, pos):
    n, d = x.shape
    nh = N_GROUP_HEADS
    wqm, wkm, wvm, wqf, wkf, wvf, wfl = _split_w_in(w_in)
    w = jnp.concatenate([wqm * Q_SCALE, wkm, wvm, wqf * Q_SCALE, wkf, wvf,
                         jnp.pad(wfl, ((0, 0), (0, LANES - nh)))], axis=1).astype(BF16)
    bf = jnp.pad(b_forget, (0, LANES - nh)).reshape(1, LANES)
    tab, _ = _rope_tables(pos)
    full = lambda shape: pl.BlockSpec(shape, lambda i: (0,) * len(shape))
    wide = jax.ShapeDtypeStruct((n, GROUP_W), F32)
    return pl.pallas_call(
        _inproj_sample_kernel,
        grid=(1,),
        in_specs=[full((n, d)), full((n, d)), full((n, d)), full((1, d)), full(w.shape),
                  full((1, LANES)), full(tab.shape)],
        out_specs=[full((n, GROUP_W))] * 6 + [full((n, LANES))],
        out_shape=[wide] * 6 + [jax.ShapeDtypeStruct((n, LANES), F32)],
        compiler_params=_cparams(("arbitrary",)),
        name="inproj_sample",
    )(x, shift, scale, g_attn.reshape(1, d), w, bf, tab)


def _expand_heads(a):
    n = a.shape[0]
    return jnp.concatenate([jnp.broadcast_to(a[hh:hh + 1], (SUBLANES, a.shape[1])) for hh in range(n)],
                           axis=0)


def _dot3r(x, b01):
    hi, mid, lo = _split3(x)
    d = functools.partial(jnp.dot, preferred_element_type=F32)
    return d(hi, b01) + d(mid, b01) + d(lo, b01)


def _online_update(s, v, m_ref, l_ref, acc_ref):
    m_old = m_ref[...]
    m_new = jnp.maximum(m_old, jnp.max(s, axis=1, keepdims=True))
    alpha = jnp.exp2(m_old - m_new)
    p = jnp.exp2(s - m_new)
    l_ref[...] = alpha * l_ref[...] + jnp.sum(p, axis=1, keepdims=True)
    acc_ref[...] = alpha * acc_ref[...] + jnp.dot(p.astype(BF16), v, preferred_element_type=F32)
    m_ref[...] = m_new


def _sample_attn_kernel(pt_ref, qm_ref, qf_ref, knm_ref, vnm_ref, knf_ref, vnf_ref, lnew_ref, rexp_ref,
                        km_ref, vm_ref, kf_ref, vf_ref, lf_ref,
                        om_ref, of_ref,
                        mf_ref, lfs_ref, accf_ref, csuf_ref,
                        mb_ref, lb_ref, accb_ref, ksum_ref, gates_ref, mstat_ref, lstat_ref, ob_ref,
                        *, n_pages, rows, pages_per_block):
    del pt_ref
    j = pl.program_id(1)
    jj = n_pages - 1 - j
    nblk = n_pages // pages_per_block
    page, nh, hd = km_ref.shape
    cols = page * nh
    col = lax.broadcasted_iota(jnp.int32, (rows, cols), 1)
    rowc = lax.broadcasted_iota(jnp.int32, (rows, cols), 0)
    same_head = jnp.bitwise_and(col, nh - 1) == jnp.right_shift(rowc, 3)
    lane = lax.broadcasted_iota(jnp.int32, (rows, LANES), 1)
    trow = jnp.bitwise_and(lax.broadcasted_iota(jnp.int32, (rows, LANES), 0), SUBLANES - 1)
    flat = lambda ref: ref[...].reshape(cols, hd).astype(BF16)

    @pl.when(j == 0)
    def _():
        mf_ref[...] = jnp.full_like(mf_ref, -jnp.inf)
        lfs_ref[...] = jnp.zeros_like(lfs_ref)
        accf_ref[...] = jnp.zeros_like(accf_ref)
        csuf_ref[...] = jnp.zeros_like(csuf_ref)
        gates_ref[...] = jnp.zeros_like(gates_ref)
        mstat_ref[...] = jnp.zeros_like(mstat_ref)
        lstat_ref[...] = jnp.zeros_like(lstat_ref)

    @pl.when(jj % pages_per_block == pages_per_block - 1)
    def _():
        mb_ref[...] = jnp.full_like(mb_ref, -jnp.inf)
        lb_ref[...] = jnp.zeros_like(lb_ref)
        accb_ref[...] = jnp.zeros_like(accb_ref)
        ksum_ref[...] = jnp.zeros_like(ksum_ref)

    qm = qm_ref[...]
    qm_b = qm.astype(BF16)
    qf_b = qf_ref[...].astype(BF16)
    rexp = rexp_ref[...]

    a_new = jnp.where(lane <= trow, lnew_ref[...], 0.0)
    cq = jnp.sum(a_new, axis=1, keepdims=True)
    kr = lax.broadcasted_iota(jnp.int32, (page, page), 0)
    kc = lax.broadcasted_iota(jnp.int32, (page, page), 1)
    tri_after = jnp.where(kr > kc, 1.0, 0.0).astype(BF16)

    lexp = _expand_heads(lf_ref[...])
    decay = _dot3r(lexp, tri_after) + csuf_ref[...] + cq
    s_f = _nt_dot(qf_b, flat(kf_ref)) + _dot3r(decay * LOG2E, rexp)
    _online_update(jnp.where(same_head, s_f, -jnp.inf), flat(vf_ref), mf_ref, lfs_ref, accf_ref)
    csuf_ref[...] += jnp.sum(lexp, axis=1, keepdims=True)

    s_m = jnp.where(same_head, _nt_dot(qm_b, flat(km_ref)), -jnp.inf)
    _online_update(s_m, flat(vm_ref), mb_ref, lb_ref, accb_ref)
    ksum_ref[...] += jnp.sum(km_ref[...], axis=0)

    @pl.when(jj % pages_per_block == 0)
    def _():
        n = jj // pages_per_block
        kmean = _expand_heads(ksum_ref[...] * (1.0 / (page * pages_per_block)))
        gate = jnp.sum(qm * kmean, axis=1, keepdims=True)
        hit = lane == n
        gates_ref[...] = jnp.where(hit, gate, gates_ref[...])
        mstat_ref[...] = jnp.where(hit, mb_ref[...], mstat_ref[...])
        lstat_ref[...] = jnp.where(hit, lb_ref[...], lstat_ref[...])
        ob_ref[n] = accb_ref[...]

    @pl.when(j == n_pages - 1)
    def _():
        causal = same_head & (jnp.right_shift(col, 3) <= jnp.bitwise_and(rowc, SUBLANES - 1))
        bias_new = _dot3r(_dot3r(a_new, tri_after) * LOG2E, rexp)
        s_new = _nt_dot(qf_b, flat(knf_ref)) + bias_new
        _online_update(jnp.where(causal, s_new, -jnp.inf), flat(vnf_ref), mf_ref, lfs_ref, accf_ref)
        of_ref[...] = accf_ref[...] / lfs_ref[...]

        s_own = jnp.where(causal, _nt_dot(qm_b, flat(knm_ref)), -jnp.inf)
        m_own = jnp.max(s_own, axis=1, keepdims=True)
        p_own = jnp.exp2(s_own - m_own)
        l_own = jnp.sum(p_own, axis=1, keepdims=True)
        o_own = jnp.dot(p_own.astype(BF16), flat(vnm_ref), preferred_element_type=F32)

        valid = lane < nblk
        g = jnp.where(valid, gates_ref[...], -jnp.inf)
        sel = lane < 0
        for _ in range(MOBA_TOPK):
            mx = jnp.max(g, axis=1, keepdims=True)
            first = jnp.min(jnp.where(g == mx, lane, LANES), axis=1, keepdims=True)
            pick = lane == first
            sel = sel | (pick & valid)
            g = jnp.where(pick, -jnp.inf, g)
        mstat = mstat_ref[...]
        m_all = jnp.maximum(m_own, jnp.max(jnp.where(sel, mstat, -jnp.inf), axis=1, keepdims=True))
        wgt = jnp.where(sel, jnp.exp2(mstat - m_all), 0.0)
        w_own = jnp.exp2(m_own - m_all)
        denom = jnp.sum(wgt * lstat_ref[...], axis=1, keepdims=True) + w_own * l_own

        def body(n, numer):
            wcol = jnp.sum(jnp.where(lane == n, wgt, 0.0), axis=1, keepdims=True)
            return numer + wcol * ob_ref[n]

        numer = lax.fori_loop(0, nblk, body, w_own * o_own)
        om_ref[...] = numer / denom


def _sample_attention(page_table, q_m, q_f, knew_m, vnew_m, knew_f, vnew_f, lnew,
                      ck_m, cv_m, ck_f, cv_f, clf_t):
    nseq, rows, hd = q_m.shape
    n_pages = page_table.shape[1]
    page, nh = ck_m.shape[1], ck_m.shape[2]
    pages_per_block = MOBA_BLOCK // page
    nblk = n_pages // pages_per_block
    pt_flat = page_table.reshape(-1)
    rexp = jnp.asarray(np.repeat(np.eye(page, dtype=np.float32), nh, axis=1), BF16)

    seq = lambda shape: pl.BlockSpec((None,) + shape, lambda b, j, pt: (b,) + (0,) * len(shape))
    pg = lambda shape: pl.BlockSpec(
        (None,) + shape, lambda b, j, pt: (pt[b * n_pages + n_pages - 1 - j],) + (0,) * len(shape))
    col = lambda: pltpu.VMEM((rows, 1), F32)
    acc = lambda: pltpu.VMEM((rows, hd), F32)
    stat = lambda: pltpu.VMEM((rows, LANES), F32)
    grid_spec = pltpu.PrefetchScalarGridSpec(
        num_scalar_prefetch=1,
        grid=(nseq, n_pages),
        in_specs=[seq((rows, hd)), seq((rows, hd)), seq((page, nh, hd)), seq((page, nh, hd)),
                  seq((page, nh, hd)), seq((page, nh, hd)), seq((rows, LANES)),
                  pl.BlockSpec(rexp.shape, lambda b, j, pt: (0, 0)),
                  pg((page, nh, hd)), pg((page, nh, hd)), pg((page, nh, hd)), pg((page, nh, hd)),
                  pg((nh, page))],
        out_specs=[seq((rows, hd)), seq((rows, hd))],
        scratch_shapes=[col(), col(), acc(), col(),
                        col(), col(), acc(), pltpu.VMEM((nh, hd), F32),
                        stat(), stat(), stat(), pltpu.VMEM((nblk, rows, hd), F32)])
    return pl.pallas_call(
        functools.partial(_sample_attn_kernel, n_pages=n_pages, rows=rows,
                          pages_per_block=pages_per_block),
        grid_spec=grid_spec,
        out_shape=[jax.ShapeDtypeStruct((nseq, rows, hd), F32)] * 2,
        compiler_params=_cparams(("arbitrary", "arbitrary")),
        name="sample_attention",
    )(pt_flat, q_m, q_f, knew_m, vnew_m, knew_f, vnew_f, lnew, rexp, ck_m, cv_m, ck_f, cv_f, clf_t)


def _sample_mixers(x, mod, g_attn, w_in, b_forget, caches, page_table):
    nseq, t, d = x.shape
    nh = N_GROUP_HEADS
    n = nseq * t
    ck_m, cv_m, ck_f, cv_f, clf = caches
    page = ck_m.shape[1]
    past_len = page_table.shape[1] * page
    pos = jnp.tile(past_len + jnp.arange(t, dtype=jnp.int32), nseq)
    per_tok = lambda a: jnp.broadcast_to(a[:, None, :], (nseq, t, d)).reshape(n, d)
    qm, km, vm, qf, kf, vf, logf = _inproj_sample(
        x.reshape(n, d), per_tok(mod[:, 0]), per_tok(mod[:, 1]), g_attn, w_in, b_forget, pos)
    shp = (nseq, t, nh, HEAD_DIM)
    q_rows = lambda a: jnp.swapaxes(a.reshape(shp), 1, 2).reshape(nseq, nh * t, HEAD_DIM)
    new_page = lambda a: jnp.pad(a.reshape(shp), ((0, 0), (0, page - t), (0, 0), (0, 0)))
    lf_new = logf[:, 0:nh].reshape(nseq, t, nh)
    lnew = jnp.pad(jnp.repeat(jnp.swapaxes(lf_new, 1, 2), t, axis=1), ((0, 0), (0, 0), (0, LANES - t)))
    om, of = _sample_attention(
        page_table, q_rows(qm), q_rows(qf), new_page(km), new_page(vm), new_page(kf), new_page(vf), lnew,
        ck_m, cv_m, ck_f, cv_f, jnp.swapaxes(clf, 1, 2))
    tok_major = lambda o: jnp.swapaxes(o.reshape(nseq, nh, t, HEAD_DIM), 1, 2).reshape(n, GROUP_W)
    otm = tok_major(om).T[None]
    otf = tok_major(of).T[None]
    new_rows = (km.reshape(shp), vm.reshape(shp), kf.reshape(shp), vf.reshape(shp), lf_new)
    return otm, otf, new_rows


def _prompt_mixers(x, mod, g_attn, w_in, b_forget):
    b, s, d = x.shape
    (km, vm, kf, vf, logf, kpm, kpf, qtm, qtg, qtf, fqt, vtm, vtf, kmean) = _inproj_prompt(
        x, mod[:, 0:1], mod[:, 1:2], g_attn, w_in, b_forget, ts=MOBA_BLOCK)
    bias = _moba_gate(qtg, kmean, tg=min(s, 2048))
    otm = _attention(qtm, bias, kpm, vtm, MOBA_EXT_ROWS, MOBA_BLOCK)
    otf = _attention(qtf, fqt, kpf, vtf, FOX_EXT_ROWS, MOBA_BLOCK)
    nh = N_GROUP_HEADS
    new_rows = (km.reshape(b, s, nh, HEAD_DIM), vm.reshape(b, s, nh, HEAD_DIM),
                kf.reshape(b, s, nh, HEAD_DIM), vf.reshape(b, s, nh, HEAD_DIM), logf[:, :, 0:nh])
    return otm, otf, new_rows


def _peer_ffn(h2, x1, gate, peer_tables, g_final, tt_route, tt, te, tokens_per_gate):
    wqt, subkeys, u_bf, vt_bf = peer_tables
    route = _peer_route(h2, wqt, subkeys, tt_route)
    return _peer_experts(h2, u_bf, vt_bf, route, x1, gate, g_final, tt, te, tokens_per_gate)


def kernel(x_prompt, x_sample, c_prompt, c_sample, cache_k_moba, cache_v_moba, cache_k_fox, cache_v_fox, cache_logf_fox, page_table, w_ada, b_ada, g_attn, g_ffn, w_in, b_forget, g_out_moba, g_out_fox, w_out, peer_wq, peer_subkeys, peer_u, peer_v, g_final):
    b, s, d = x_prompt.shape
    nseq, t, _ = x_sample.shape
    depth = w_ada.shape[0]
    assert depth == 1 and t == SUBLANES and s % MOBA_BLOCK == 0 and s // MOBA_BLOCK <= MOBA_EXT_ROWS
    l = 0

    c_all = jnp.concatenate([c_prompt, c_sample], axis=0)
    pad = -c_all.shape[0] % SUBLANES
    mod = _modulation(jnp.pad(c_all, ((0, pad), (0, 0))), w_ada[l], b_ada[l])
    mod = mod[:b + nseq].reshape(b + nseq, N_ADA, d)
    mp, ms = mod[:b], mod[b:]

    peer_tables = (peer_wq[l].T.astype(BF16),
                   peer_subkeys[l].reshape(PEER_HEADS * 2, PEER_NKEYS, PEER_HALF),
                   peer_u[l].astype(BF16), peer_v[l].T.astype(BF16))

    otm, otf, rows_p = _prompt_mixers(x_prompt, mp, g_attn[l], w_in[l], b_forget[l])
    x1, h2 = _merge(x_prompt, otm, otf, mp[:, 2:3], mp[:, 3:4], mp[:, 4:5],
                    g_out_moba[l], g_out_fox[l], g_ffn[l], w_out[l], tm=min(s, 512))
    y_prompt = _peer_ffn(h2.reshape(b * s, d), x1.reshape(b * s, d), mp[:, 5:6], peer_tables, g_final,
                         tt_route=LANES, tt=min(s, 512), te=1024, tokens_per_gate=s).reshape(b, s, d)

    n = nseq * t
    per_tok = lambda a: jnp.broadcast_to(a[:, None, :], (nseq, t, d)).reshape(1, n, d)
    caches = (cache_k_moba[l], cache_v_moba[l], cache_k_fox[l], cache_v_fox[l], cache_logf_fox[l])
    otm_s, otf_s, rows_s = _sample_mixers(x_sample, ms, g_attn[l], w_in[l], b_forget[l], caches, page_table)
    x1s, h2s = _merge(x_sample.reshape(1, n, d), otm_s, otf_s, per_tok(ms[:, 2]), per_tok(ms[:, 3]),
                      per_tok(ms[:, 4]), g_out_moba[l], g_out_fox[l], g_ffn[l], w_out[l], tm=n)
    y_sample = _peer_ffn(h2s.reshape(n, d), x1s.reshape(n, d), per_tok(ms[:, 5]), peer_tables, g_final,
                         tt_route=min(n, LANES), tt=n, te=1024, tokens_per_gate=n).reshape(nseq, t, d)

    stack = lambda a: a[None]
    return (y_prompt, y_sample) + tuple(stack(a) for a in rows_p) + tuple(stack(a) for a in rows_s)
```

```python
import functools
import math

import numpy as np
import jax
import jax.numpy as jnp
from jax import lax
from jax.experimental import pallas as pl
from jax.experimental.pallas import tpu as pltpu

F32 = jnp.float32
BF16 = jnp.bfloat16
HIGHEST = lax.Precision.HIGHEST

HEAD_DIM = 64
N_GROUP_HEADS = 8
GROUP_W = N_GROUP_HEADS * HEAD_DIM
MOBA_BLOCK = 256
MOBA_BLOCK_LOG2 = 8
MOBA_TOPK = 3
ROPE_THETA = 500000.0
ROPE_DIM = HEAD_DIM // 4
ROPE_HALF = ROPE_DIM // 2
PEER_HEADS = 8
PEER_NKEYS = 128
PEER_HALF = 128
PEER_TOPK = 16
N_ADA = 6
EPS = 1e-6

LANES = 128
SUBLANES = 8
VMEM_LIMIT_BYTES = 56 * 1024 * 1024

LOG2E = math.log2(math.e)
Q_SCALE = HEAD_DIM ** -0.5 * LOG2E
NEG_BIG = -1.0e30
PAD_W = 2 * HEAD_DIM
FOX_EXT_ROWS = 16
MOBA_EXT_ROWS = 32


def _cparams(sem):
    return pltpu.CompilerParams(dimension_semantics=sem, vmem_limit_bytes=VMEM_LIMIT_BYTES)


def _nt_dot(a, b):
    return lax.dot_general(a, b, (((1,), (1,)), ((), ())), preferred_element_type=F32)


def _split3(x):
    hi = x.astype(BF16)
    r1 = x - hi.astype(F32)
    mid = r1.astype(BF16)
    lo = (r1 - mid.astype(F32)).astype(BF16)
    return hi, mid, lo


def _dot3(a01, x):
    hi, mid, lo = _split3(x)
    d = functools.partial(jnp.dot, preferred_element_type=F32)
    return d(a01, hi) + d(a01, mid) + d(a01, lo)


def _log_sigmoid(z):
    return jnp.minimum(z, 0.0) - jnp.log1p(jnp.exp(-jnp.abs(z)))


def _rms_scale(x):
    return lax.rsqrt(jnp.mean(x * x, axis=-1, keepdims=True) + EPS)


def _mod_kernel(c_ref, w_ref, b_ref, o_ref):
    c = c_ref[...]
    s = c / (1.0 + jnp.exp(-c))
    o_ref[...] = jnp.dot(s, w_ref[...], precision=HIGHEST, preferred_element_type=F32) + b_ref[...]


def _modulation(c_all, w_ada, b_ada):
    m, d = c_all.shape
    n = w_ada.shape[1]
    tn = 1536
    return pl.pallas_call(
        _mod_kernel,
        grid=(n // tn,),
        in_specs=[pl.BlockSpec((m, d), lambda j: (0, 0)),
                  pl.BlockSpec((d, tn), lambda j: (0, j)),
                  pl.BlockSpec((1, tn), lambda j: (0, j))],
        out_specs=pl.BlockSpec((m, tn), lambda j: (0, j)),
        out_shape=jax.ShapeDtypeStruct((m, n), F32),
        compiler_params=_cparams(("arbitrary",)),
        name="modulation",
    )(c_all, w_ada, b_ada.reshape(1, n))


def _rope_tables(pos):
    inv = ROPE_THETA ** (-jnp.arange(ROPE_HALF, dtype=F32) / ROPE_HALF)
    ang = pos.astype(F32)[:, None] * inv[None, :]
    cos, sin = jnp.cos(ang), jnp.sin(ang)
    t = pos.shape[0]
    one = jnp.ones((t, HEAD_DIM - ROPE_DIM), F32)
    zero = jnp.zeros((t, HEAD_DIM - ROPE_DIM), F32)
    z8 = jnp.zeros((t, ROPE_HALF), F32)
    c = jnp.concatenate([cos, cos, one], axis=1)
    sm = jnp.concatenate([-sin, z8, zero], axis=1)
    sp = jnp.concatenate([z8, sin, zero], axis=1)
    nat = jnp.stack([jnp.tile(a, (1, 2)) for a in (c, sm, sp)])
    tr = jnp.stack([cos.T, sin.T])
    return nat, tr


def _rope_nat(x, tab_ref):
    xm = pltpu.roll(x, LANES - ROPE_HALF, axis=1)
    xp = pltpu.roll(x, ROPE_HALF, axis=1)
    return x * tab_ref[0] + xm * tab_ref[1] + xp * tab_ref[2]


def _inproj_kernel(x_ref, shift_ref, scale_ref, g_ref, wnat_ref, wt_ref, bf_ref, tab_ref, tabt_ref,
                   ltri_ref, pk_ref, pq_ref,
                   km_ref, vm_ref, kf_ref, vf_ref, logf_ref, kpm_ref, kpf_ref,
                   qtm_ref, qtg_ref, qtf_ref, fqt_ref, vtm_ref, vtf_ref, kmean_ref,
                   carry_ref, *, ts):
    j = pl.program_id(1)
    nh = N_GROUP_HEADS
    x = x_ref[...]
    h = (x * _rms_scale(x) * g_ref[...]) * (1.0 + scale_ref[...]) + shift_ref[...]
    hb = h.astype(BF16)
    pn = jnp.dot(hb, wnat_ref[...], preferred_element_type=F32)
    pt = _nt_dot(wt_ref[...], hb)
    w_sp = nh * PAD_W
    kpm = pn[:, 0:w_sp]
    kpf = pn[:, w_sp:2 * w_sp]
    vm_ref[...] = pn[:, 2 * w_sp:2 * w_sp + GROUP_W]
    vf_ref[...] = pn[:, 2 * w_sp + GROUP_W:2 * w_sp + 2 * GROUP_W]
    fl = pn[:, 2 * w_sp + 2 * GROUP_W:2 * w_sp + 2 * GROUP_W + LANES]

    lane = lax.broadcasted_iota(jnp.int32, (ts, PAD_W), 1)
    row = lax.broadcasted_iota(jnp.int32, (ts, PAD_W), 0)
    blk = jnp.right_shift(j * ts + row, MOBA_BLOCK_LOG2)
    onehot = lane == HEAD_DIM + blk
    km_heads = [_rope_nat(kpm[:, hh * PAD_W:(hh + 1) * PAD_W], tab_ref) for hh in range(nh)]
    nblk = ts // MOBA_BLOCK
    for hh in range(nh):
        kh = km_heads[hh]
        kpm_ref[:, hh * PAD_W:(hh + 1) * PAD_W] = jnp.where(onehot, 1.0, kh).astype(BF16)
        for bb in range(nblk):
            kmean_ref[bb, :, hh * PAD_W:(hh + 1) * PAD_W] = jnp.mean(
                kh[bb * MOBA_BLOCK:(bb + 1) * MOBA_BLOCK], axis=0, keepdims=True)
    for pp in range(nh // 2):
        km_ref[:, pp * PAD_W:(pp + 1) * PAD_W] = (
            km_heads[2 * pp] + pltpu.roll(km_heads[2 * pp + 1], HEAD_DIM, axis=1))
        kf_ref[:, pp * PAD_W:(pp + 1) * PAD_W] = (
            kpf[:, 2 * pp * PAD_W:(2 * pp + 1) * PAD_W]
            + pltpu.roll(kpf[:, (2 * pp + 1) * PAD_W:(2 * pp + 2) * PAD_W], HEAD_DIM, axis=1))

    lane1 = lax.broadcasted_iota(jnp.int32, (ts, LANES), 1)
    logf = jnp.where(lane1 < nh, _log_sigmoid(fl + bf_ref[...]), 0.0)
    logf_ref[...] = logf

    @pl.when(j == 0)
    def _():
        carry_ref[...] = jnp.zeros_like(carry_ref)

    cum = _dot3(ltri_ref[...], logf) + carry_ref[0:1, :]
    carry_ref[0:1, :] = cum[ts - 1:ts, :]
    g = cum * LOG2E
    g_hi, g_mid, g_lo = _split3(g)
    parts = (g_hi.astype(F32) + pltpu.roll(g_mid.astype(F32), nh, axis=1)
             + pltpu.roll(g_lo.astype(F32), 2 * nh, axis=1)
             + jnp.where(lane1 == 3 * nh, 1.0, 0.0)).astype(BF16)
    kpf_ref[...] = (kpf + jnp.dot(parts, pk_ref[...], preferred_element_type=F32)).astype(BF16)
    fqt_ref[...] = _nt_dot(pq_ref[...], parts).astype(BF16)

    qtm = pt[0:GROUP_W]
    cos_t = tabt_ref[0]
    sin_t = tabt_ref[1]
    pieces = []
    for hh in range(nh):
        base = hh * HEAD_DIM
        x1 = qtm[base:base + ROPE_HALF]
        x2 = qtm[base + ROPE_HALF:base + ROPE_DIM]
        pieces += [x1 * cos_t - x2 * sin_t, x2 * cos_t + x1 * sin_t, qtm[base + ROPE_DIM:base + HEAD_DIM]]
    qtm = jnp.concatenate(pieces, axis=0)
    qtg_ref[...] = qtm
    qtm_ref[...] = qtm.astype(BF16)
    qtf_ref[...] = pt[GROUP_W:2 * GROUP_W].astype(BF16)
    vtm = pt[2 * GROUP_W:3 * GROUP_W].astype(BF16)
    vtf = pt[3 * GROUP_W:4 * GROUP_W].astype(BF16)
    for bb in range(nblk):
        sl = slice(bb * MOBA_BLOCK, (bb + 1) * MOBA_BLOCK)
        for hh in range(nh):
            vtm_ref[hh, bb] = vtm[hh * HEAD_DIM:(hh + 1) * HEAD_DIM, sl]
            vtf_ref[hh, bb] = vtf[hh * HEAD_DIM:(hh + 1) * HEAD_DIM, sl]


def _spread_heads(w):
    d = w.shape[0]
    w = w.reshape(d, N_GROUP_HEADS, HEAD_DIM)
    return jnp.pad(w, ((0, 0), (0, 0), (0, PAD_W - HEAD_DIM))).reshape(d, N_GROUP_HEADS * PAD_W)


def _split_w_in(w_in):
    cuts = np.cumsum([GROUP_W] * 6).tolist()
    return jnp.split(w_in, cuts, axis=1)


def _fox_placement():
    nh = N_GROUP_HEADS
    pk = np.zeros((LANES, nh * PAD_W), np.float32)
    pq = np.zeros((nh * FOX_EXT_ROWS, LANES), np.float32)
    for hh in range(nh):
        for t in range(3):
            pk[t * nh + hh, hh * PAD_W + HEAD_DIM + t] = -1.0
            pk[3 * nh, hh * PAD_W + HEAD_DIM + 3 + t] = 1.0
            pq[hh * FOX_EXT_ROWS + t, 3 * nh] = 1.0
            pq[hh * FOX_EXT_ROWS + 3 + t, t * nh + hh] = 1.0
    return jnp.asarray(pk, BF16), jnp.asarray(pq, BF16)


def _inproj_prompt(x, shift, scale, g_attn, w_in, b_forget, ts):
    b, s, d = x.shape
    nh = N_GROUP_HEADS
    nb = s // MOBA_BLOCK
    wqm, wkm, wvm, wqf, wkf, wvf, wfl = _split_w_in(w_in)
    wnat = jnp.concatenate([_spread_heads(wkm), _spread_heads(wkf), wvm, wvf,
                            jnp.pad(wfl, ((0, 0), (0, LANES - nh)))], axis=1).astype(BF16)
    wt = jnp.concatenate([wqm * Q_SCALE, wqf * Q_SCALE, wvm, wvf], axis=1).T.astype(BF16)
    bf = jnp.pad(b_forget, (0, LANES - nh)).reshape(1, LANES)
    tab, tabt = _rope_tables(jnp.arange(s, dtype=jnp.int32))
    ltri = jnp.tril(jnp.ones((ts, ts), F32)).astype(BF16)
    pk, pq = _fox_placement()
    nblk = ts // MOBA_BLOCK
    w_sp = nh * PAD_W

    def tok(width, dtype):
        return (jax.ShapeDtypeStruct((b, s, width), dtype),
                pl.BlockSpec((None, ts, width), lambda i, j: (i, j, 0)))

    def tr(rows, dtype):
        return (jax.ShapeDtypeStruct((b, rows, s), dtype),
                pl.BlockSpec((None, rows, ts), lambda i, j: (i, 0, j)))

    vt = (jax.ShapeDtypeStruct((b, nh, nb, HEAD_DIM, MOBA_BLOCK), BF16),
          pl.BlockSpec((None, nh, nblk, HEAD_DIM, MOBA_BLOCK), lambda i, j: (i, 0, j, 0, 0)))
    kmean = (jax.ShapeDtypeStruct((b, nb, 1, w_sp), F32),
             pl.BlockSpec((None, nblk, 1, w_sp), lambda i, j: (i, j, 0, 0)))
    outs = [tok(GROUP_W, F32), tok(GROUP_W, F32), tok(GROUP_W, F32), tok(GROUP_W, F32),
            tok(LANES, F32), tok(w_sp, BF16), tok(w_sp, BF16),
            tr(GROUP_W, BF16), tr(GROUP_W, F32), tr(GROUP_W, BF16), tr(nh * FOX_EXT_ROWS, BF16),
            vt, vt, kmean]
    const = lambda shape: pl.BlockSpec(shape, lambda i, j: (0,) * len(shape))
    row_spec = pl.BlockSpec((None, 1, d), lambda i, j: (i, 0, 0))
    return pl.pallas_call(
        functools.partial(_inproj_kernel, ts=ts),
        grid=(b, s // ts),
        in_specs=[pl.BlockSpec((None, ts, d), lambda i, j: (i, j, 0)), row_spec, row_spec,
                  const((1, d)), const(wnat.shape), const(wt.shape), const((1, LANES)),
                  pl.BlockSpec((3, ts, LANES), lambda i, j: (0, j, 0)),
                  pl.BlockSpec((2, ROPE_HALF, ts), lambda i, j: (0, 0, j)),
                  const((ts, ts)), const(pk.shape), const(pq.shape)],
        out_specs=[o[1] for o in outs],
        out_shape=[o[0] for o in outs],
        scratch_shapes=[pltpu.VMEM((SUBLANES, LANES), F32)],
        compiler_params=_cparams(("arbitrary", "arbitrary")),
        name="inproj_prompt",
    )(x, shift, scale, g_attn.reshape(1, d), wnat, wt, bf, tab, tabt, ltri, pk, pq)


def _first_max(g, idx, big):
    mx = jnp.max(g, axis=0, keepdims=True)
    first = jnp.min(jnp.where(g == mx, idx, big), axis=0, keepdims=True)
    return mx, first


def _moba_gate_kernel(qt_ref, kmean_ref, bias_ref, *, tg, nb):
    i = pl.program_id(2)
    gate = jnp.dot(kmean_ref[...][:, 0:HEAD_DIM], qt_ref[...], precision=HIGHEST,
                   preferred_element_type=F32)
    n = lax.broadcasted_iota(jnp.int32, (nb, tg), 0)
    qblk = jnp.right_shift(i * tg + lax.broadcasted_iota(jnp.int32, (nb, tg), 1), MOBA_BLOCK_LOG2)
    past = n < qblk
    g = jnp.where(past, gate, -jnp.inf)
    sel = n == qblk
    for _ in range(MOBA_TOPK):
        _, first = _first_max(g, n, nb)
        pick = n == first
        sel = sel | (pick & past)
        g = jnp.where(pick, -jnp.inf, g)
    bias_ref[...] = jnp.where(sel, 0.0, NEG_BIG).astype(BF16)


def _moba_gate(qtg, kmean, tg):
    b, _, s = qtg.shape
    nb = MOBA_EXT_ROWS
    kmean = kmean.reshape(b, kmean.shape[1], N_GROUP_HEADS * PAD_W)
    kmean = jnp.pad(kmean, ((0, 0), (0, nb - kmean.shape[1]), (0, 0)))
    return pl.pallas_call(
        functools.partial(_moba_gate_kernel, tg=tg, nb=nb),
        grid=(b, N_GROUP_HEADS, s // tg),
        in_specs=[pl.BlockSpec((None, HEAD_DIM, tg), lambda bi, h, i: (bi, h, i)),
                  pl.BlockSpec((None, nb, PAD_W), lambda bi, h, i: (bi, 0, h))],
        out_specs=pl.BlockSpec((None, None, nb, tg), lambda bi, h, i: (bi, h, 0, i)),
        out_shape=jax.ShapeDtypeStruct((b, N_GROUP_HEADS, nb, s), BF16),
        compiler_params=_cparams(("arbitrary", "arbitrary", "arbitrary")),
        name="moba_gate",
    )(qtg, kmean)


ATTN_HEADS_PER_STEP = 4


def _attn_kernel(qt_ref, ext_ref, kp_ref, vt_ref, o_ref, qp_ref, *, tq, ext_rows, hb):
    i = pl.program_id(2)
    ext = ext_ref[...].reshape(hb * ext_rows, tq)
    for hh in range(hb):
        qp_ref[hh, 0:HEAD_DIM, :] = qt_ref[hh * HEAD_DIM:(hh + 1) * HEAD_DIM, :]
        qp_ref[hh, HEAD_DIM:HEAD_DIM + ext_rows, :] = ext[hh * ext_rows:(hh + 1) * ext_rows]
        qp_ref[hh, HEAD_DIM + ext_rows:PAD_W, :] = jnp.zeros((PAD_W - HEAD_DIM - ext_rows, tq), BF16)
    qps = [qp_ref[hh] for hh in range(hb)]

    def step(tiles, carry):
        s_all = [[jnp.dot(k[:, hh * PAD_W:(hh + 1) * PAD_W], qps[hh], preferred_element_type=F32)
                  for hh in range(hb)] for k, _, _ in tiles]
        out = []
        for hh in range(hb):
            m, l, acc = carry[3 * hh:3 * hh + 3]
            s_h = [s_all[t][hh] if mask is None else jnp.where(mask, s_all[t][hh], -jnp.inf)
                   for t, (_, _, mask) in enumerate(tiles)]
            m_new = m
            for s_t in s_h:
                m_new = jnp.maximum(m_new, jnp.max(s_t, axis=0, keepdims=True))
            alpha = jnp.exp2(m - m_new)
            l = alpha * l
            acc = alpha * acc
            for s_t, (_, vidx, _) in zip(s_h, tiles):
                p = jnp.exp2(s_t - m_new)
                l = l + jnp.sum(p, axis=0, keepdims=True)
                acc = acc + jnp.dot(vt_ref[hh, vidx], p.astype(BF16), preferred_element_type=F32)
            out += [m_new, l, acc]
        return tuple(out)

    def keys(n):
        return kp_ref[pl.ds(pl.multiple_of(n * tq, tq), tq), :]

    kpos = lax.broadcasted_iota(jnp.int32, (tq, tq), 0)
    qpos = lax.broadcasted_iota(jnp.int32, (tq, tq), 1)
    odd = i % 2
    init = []
    for hh in range(hb):
        init += [jnp.full((1, tq), -jnp.inf, F32), jnp.zeros((1, tq), F32), jnp.zeros((HEAD_DIM, tq), F32)]
    carry = step([(keys(i), i, kpos <= qpos), (keys(0), 0, kpos < odd * tq)], tuple(init))

    def body(u, carry):
        n = odd + 2 * u
        return step([(keys(n), n, None), (keys(n + 1), n + 1, None)], carry)

    fin = lax.fori_loop(0, i // 2, body, carry)
    for hh in range(hb):
        o_ref[hh * HEAD_DIM:(hh + 1) * HEAD_DIM, :] = fin[3 * hh + 2] / fin[3 * hh + 1]


def _attention(qt, ext, kp, vt, ext_rows, tq):
    b, _, s = qt.shape
    hb = ATTN_HEADS_PER_STEP
    ng = N_GROUP_HEADS // hb
    nb = s // tq
    if ext.ndim == 4:
        ext_spec = pl.BlockSpec((None, hb, ext_rows, tq), lambda bi, g, i: (bi, g, 0, i))
    else:
        ext_spec = pl.BlockSpec((None, hb * ext_rows, tq), lambda bi, g, i: (bi, g, i))
    return pl.pallas_call(
        functools.partial(_attn_kernel, tq=tq, ext_rows=ext_rows, hb=hb),
        grid=(b, ng, nb),
        in_specs=[pl.BlockSpec((None, hb * HEAD_DIM, tq), lambda bi, g, i: (bi, g, i)),
                  ext_spec,
                  pl.BlockSpec((None, s, hb * PAD_W), lambda bi, g, i: (bi, 0, g)),
                  pl.BlockSpec((None, hb, nb, HEAD_DIM, tq), lambda bi, g, i: (bi, g, 0, 0, 0))],
        out_specs=pl.BlockSpec((None, hb * HEAD_DIM, tq), lambda bi, g, i: (bi, g, i)),
        out_shape=jax.ShapeDtypeStruct((b, GROUP_W, s), F32),
        scratch_shapes=[pltpu.VMEM((hb, PAD_W, tq), BF16)],
        compiler_params=_cparams(("arbitrary", "arbitrary", "arbitrary")),
        name="attention",
    )(qt, ext, kp, vt)


def _merge_kernel(x_ref, otm_ref, otf_ref, gate_ref, shift_ref, scale_ref, gm_ref, gf_ref, gffn_ref,
                  wout_ref, x1_ref, h2_ref):
    def group_norm(ot, gcol):
        r = lax.rsqrt(jnp.mean(ot * ot, axis=0, keepdims=True) + EPS)
        return (ot * r * gcol).T

    on = jnp.concatenate([group_norm(otm_ref[...], gm_ref[...]),
                          group_norm(otf_ref[...], gf_ref[...])], axis=1).astype(BF16)
    y = jnp.dot(on, wout_ref[...], preferred_element_type=F32)
    x1 = x_ref[...] + gate_ref[...] * y
    x1_ref[...] = x1
    h2 = (x1 * _rms_scale(x1) * gffn_ref[...]) * (1.0 + scale_ref[...]) + shift_ref[...]
    h2_ref[...] = h2.astype(BF16)


def _mod_spec(arr, tm, d):
    if arr.shape[1] == 1:
        return pl.BlockSpec((None, 1, d), lambda i, j: (i, 0, 0))
    return pl.BlockSpec((None, tm, d), lambda i, j: (i, j, 0))


def _merge(x, otm, otf, gate, shift, scale, g_m, g_f, g_ffn, w_out, tm):
    b, s, d = x.shape
    tokspec = pl.BlockSpec((None, tm, d), lambda i, j: (i, j, 0))
    otspec = pl.BlockSpec((None, GROUP_W, tm), lambda i, j: (i, 0, j))
    const = lambda shape: pl.BlockSpec(shape, lambda i, j: (0,) * len(shape))
    return pl.pallas_call(
        _merge_kernel,
        grid=(b, s // tm),
        in_specs=[tokspec, otspec, otspec, _mod_spec(gate, tm, d), _mod_spec(shift, tm, d),
                  _mod_spec(scale, tm, d), const((GROUP_W, 1)), const((GROUP_W, 1)), const((1, d)),
                  const((2 * GROUP_W, d))],
        out_specs=[tokspec, tokspec],
        out_shape=[jax.ShapeDtypeStruct((b, s, d), F32), jax.ShapeDtypeStruct((b, s, d), BF16)],
        compiler_params=_cparams(("arbitrary", "arbitrary")),
        name="merge_outproj",
    )(x, otm, otf, gate, shift, scale, g_m.reshape(GROUP_W, 1), g_f.reshape(GROUP_W, 1),
      g_ffn.reshape(1, d), w_out.astype(BF16))


_CAND_GROUPS = [(0, 16), (1, 8), (2, 5), (3, 4), (4, 3), (5, 2), (6, 2), (7, 2)]
_CAND_ROWS = 16 + 7 * 8 + 8


def _top16(s, tt):
    idx = lax.broadcasted_iota(jnp.int32, (PEER_NKEYS, tt), 0)
    r16 = lax.broadcasted_iota(jnp.int32, (PEER_TOPK, tt), 0)

    def body(r, carry):
        g, rank, sv = carry
        mx, first = _first_max(g, idx, PEER_NKEYS)
        pick = idx == first
        rank = jnp.where(pick, lax.convert_element_type(r, F32), rank)
        g = jnp.where(pick, -jnp.inf, g)
        sv = jnp.where(r16 == r, mx, sv)
        return g, rank, sv

    _, rank, sv = lax.fori_loop(
        0, PEER_TOPK, body,
        (s, jnp.full((PEER_NKEYS, tt), float(PEER_TOPK), F32), jnp.zeros((PEER_TOPK, tt), F32)))
    return sv, rank


def _top16_no_ties(s, tt):
    r16 = lax.broadcasted_iota(jnp.int32, (PEER_TOPK, tt), 0)

    def body(r, carry):
        g, rank, sv = carry
        mx = jnp.max(g, axis=0, keepdims=True)
        hit = g == mx
        rank = jnp.where(hit, lax.convert_element_type(r, F32), rank)
        g = jnp.where(hit, -jnp.inf, g)
        sv = jnp.where(r16 == r, mx, sv)
        return g, rank, sv

    _, rank, sv = lax.fori_loop(
        0, PEER_TOPK, body,
        (s, jnp.full((PEER_NKEYS, tt), float(PEER_TOPK), F32), jnp.zeros((PEER_TOPK, tt), F32)))
    taken = jnp.sum(jnp.where(rank < float(PEER_TOPK), 1.0, 0.0), axis=0, keepdims=True)
    return sv, rank, jnp.max(taken) > float(PEER_TOPK)


def _cand_rows(sv0, sv1, tt):
    groups = []
    for a, cnt in _CAND_GROUPS:
        rows = 16 if a == 0 else SUBLANES
        blk = sv0[a:a + 1] + sv1[0:rows]
        if cnt < rows:
            blk = jnp.where(lax.broadcasted_iota(jnp.int32, (rows, tt), 0) < cnt, blk, -jnp.inf)
        groups.append(blk)
    groups.append(sv0[8:16] + sv1[0:1])
    return jnp.concatenate(groups, axis=0)


def _cand_counts(taken):
    sums = [jnp.sum(taken[0:16], axis=0, keepdims=True)]
    sums += [jnp.sum(taken[16 + 8 * g:24 + 8 * g], axis=0, keepdims=True) for g in range(7)]
    return jnp.concatenate(sums + [taken[72:80]], axis=0)


def _cand_top16(cand, top, tt):
    crow = lax.broadcasted_iota(jnp.int32, (_CAND_ROWS, tt), 0)

    def body(r, carry):
        g, z = carry
        mx, first = _first_max(g, crow, _CAND_ROWS)
        return jnp.where(crow == first, -jnp.inf, g), z + jnp.exp(mx - top)

    g, z = lax.fori_loop(0, PEER_TOPK, body, (cand, jnp.zeros((1, tt), F32)))
    taken = jnp.where((g == -jnp.inf) & (cand > -jnp.inf), 1.0, 0.0)
    return z, _cand_counts(taken)


def _cand_top16_no_ties(cand, top, tt):
    def body(r, carry):
        g, z = carry
        mx = jnp.max(g, axis=0, keepdims=True)
        return jnp.where(g == mx, -jnp.inf, g), z + jnp.exp(mx - top)

    g, z = lax.fori_loop(0, PEER_TOPK, body, (cand, jnp.zeros((1, tt), F32)))
    taken = jnp.where((g == -jnp.inf) & (cand > -jnp.inf), 1.0, 0.0)
    n_taken = jnp.sum(taken, axis=0, keepdims=True)
    return z, _cand_counts(taken), jnp.max(n_taken) > float(PEER_TOPK)


def _route_head(s0, s1, tt):
    sv0, rank0, tie0 = _top16_no_ties(s0, tt)
    sv0, rank0 = lax.cond(tie0, lambda: _top16(s0, tt), lambda: (sv0, rank0))
    sv1, rank1, tie1 = _top16_no_ties(s1, tt)
    sv1, rank1 = lax.cond(tie1, lambda: _top16(s1, tt), lambda: (sv1, rank1))
    cand = _cand_rows(sv0, sv1, tt)
    top = sv0[0:1] + sv1[0:1]
    z, nbc, tie_c = _cand_top16_no_ties(cand, top, tt)
    z, nbc = lax.cond(tie_c, lambda: _cand_top16(cand, top, tt), lambda: (z, nbc))
    lim0 = jnp.zeros((PEER_NKEYS, tt), F32)
    for a in range(PEER_TOPK):
        lim0 = jnp.where(rank0 == float(a), nbc[a:a + 1], lim0)
    return rank1, jnp.exp(s1 - sv1[0:1]), lim0, jnp.exp(s0 - sv0[0:1]) / z


def _peer_route_kernel(h_ref, wqt_ref, subhi_ref, sublo_ref, rank1_ref, e1_ref, lim0_ref, e0_ref, *, tt):
    qt = _nt_dot(wqt_ref[...], h_ref[...])
    q_hi = qt.astype(BF16)
    q_lo = (qt - q_hi.astype(F32)).astype(BF16)
    d = functools.partial(jnp.dot, preferred_element_type=F32)
    lt = min(tt, LANES)
    for hh in range(PEER_HEADS):
        s = []
        for p in range(2):
            rows = slice((hh * 2 + p) * PEER_HALF, (hh * 2 + p + 1) * PEER_HALF)
            s_hi, s_lo = subhi_ref[hh * 2 + p], sublo_ref[hh * 2 + p]
            s.append(d(s_hi, q_hi[rows]) + (d(s_hi, q_lo[rows]) + d(s_lo, q_hi[rows])))
        for c in range(tt // lt):
            cols = slice(c * lt, (c + 1) * lt)
            rank1, e1, lim0, e0 = _route_head(s[0][:, cols], s[1][:, cols], lt)
            rank1_ref[hh, :, cols] = rank1.astype(BF16)
            e1_ref[hh, :, cols] = e1.astype(BF16)
            lim0_ref[hh, :, cols] = lim0
            e0_ref[hh, :, cols] = e0


def _peer_route(h2, wqt, subkeys, tt):
    n, d = h2.shape
    sub_hi = subkeys.astype(BF16)
    sub_lo = (subkeys - sub_hi.astype(F32)).astype(BF16)
    spec = pl.BlockSpec((PEER_HEADS, PEER_NKEYS, tt), lambda t: (0, 0, t))
    shp = lambda dt: jax.ShapeDtypeStruct((PEER_HEADS, PEER_NKEYS, n), dt)
    return pl.pallas_call(
        functools.partial(_peer_route_kernel, tt=tt),
        grid=(n // tt,),
        in_specs=[pl.BlockSpec((tt, d), lambda t: (t, 0)),
                  pl.BlockSpec(wqt.shape, lambda t: (0, 0)),
                  pl.BlockSpec(subkeys.shape, lambda t: (0, 0, 0)),
                  pl.BlockSpec(subkeys.shape, lambda t: (0, 0, 0))],
        out_specs=[spec] * 4,
        out_shape=[shp(BF16), shp(BF16), shp(F32), shp(F32)],
        compiler_params=_cparams(("arbitrary",)),
        name="peer_route",
    )(h2, wqt, sub_hi, sub_lo)


def _gelu(x):
    return 0.5 * x * (1.0 + lax.erf(x * (2.0 ** -0.5)))


def _peer_expert_kernel(h_ref, u_ref, vt_ref, rank1_ref, e1_ref, lim0_ref, e0_ref, x1_ref, gate_ref,
                        gfin_ref, y_ref, acc_ref, gt_ref, *, tt, te):
    e = pl.program_id(1)

    @pl.when(e == 0)
    def _():
        acc_ref[...] = jnp.zeros_like(acc_ref)

    a_t = _nt_dot(u_ref[...], h_ref[...])
    for ii in range(te // PEER_NKEYS):
        i = e * (te // PEER_NKEYS) + ii
        w = jnp.zeros((PEER_NKEYS, tt), BF16)
        for hh in range(PEER_HEADS):
            lim = lim0_ref[hh, pl.ds(i, 1), :].astype(BF16)
            e0 = e0_ref[hh, pl.ds(i, 1), :].astype(BF16)
            w = w + jnp.where(rank1_ref[hh] < lim, e1_ref[hh] * e0, jnp.zeros((), BF16))
        sl = slice(ii * PEER_NKEYS, (ii + 1) * PEER_NKEYS)
        gt_ref[sl, :] = w * _gelu(a_t[sl]).astype(BF16)
    acc_ref[...] += jnp.dot(vt_ref[...], gt_ref[...], preferred_element_type=F32)

    @pl.when(e == pl.num_programs(1) - 1)
    def _():
        x2 = x1_ref[...] + gate_ref[...] * acc_ref[...].T
        y_ref[...] = x2 * _rms_scale(x2) * gfin_ref[...]


def _peer_experts(h2, u_bf, vt_bf, route, x1, gate, g_final, tt, te, tokens_per_gate):
    n, d = h2.shape
    ne = u_bf.shape[0]
    rspec = pl.BlockSpec((PEER_HEADS, PEER_NKEYS, tt), lambda t, e: (0, 0, t))
    tokspec = pl.BlockSpec((tt, d), lambda t, e: (t, 0))
    if gate.shape[1] == 1:
        gspec = pl.BlockSpec((None, 1, d), lambda t, e: (t * tt // tokens_per_gate, 0, 0))
    else:
        gspec = pl.BlockSpec((None, tt, d), lambda t, e: (0, t, 0))
    return pl.pallas_call(
        functools.partial(_peer_expert_kernel, tt=tt, te=te),
        grid=(n // tt, ne // te),
        in_specs=[tokspec,
                  pl.BlockSpec((te, d), lambda t, e: (e, 0)),
                  pl.BlockSpec((d, te), lambda t, e: (0, e)),
                  rspec, rspec, rspec, rspec, tokspec, gspec,
                  pl.BlockSpec((1, d), lambda t, e: (0, 0))],
        out_specs=tokspec,
        out_shape=jax.ShapeDtypeStruct((n, d), F32),
        scratch_shapes=[pltpu.VMEM((d, tt), F32), pltpu.VMEM((te, tt), BF16)],
        compiler_params=_cparams(("arbitrary", "arbitrary")),
        name="peer_experts",
    )(h2, u_bf, vt_bf, *route, x1, gate, g_final.reshape(1, d))


def _inproj_sample_kernel(x_ref, shift_ref, scale_ref, g_ref, w_ref, bf_ref, tab_ref,
                          qm_ref, km_ref, vm_ref, qf_ref, kf_ref, vf_ref, logf_ref):
    x = x_ref[...]
    h = (x * _rms_scale(x) * g_ref[...]) * (1.0 + scale_ref[...]) + shift_ref[...]
    p = jnp.dot(h.astype(BF16), w_ref[...], preferred_element_type=F32)
    w = GROUP_W
    for pp in range(w // LANES):
        sl = slice(pp * LANES, (pp + 1) * LANES)
        qm_ref[:, sl] = _rope_nat(p[:, sl], tab_ref)
        km_ref[:, sl] = _rope_nat(p[:, w + pp * LANES:w + (pp + 1) * LANES], tab_ref)
    vm_ref[...] = p[:, 2 * w:3 * w]
    qf_ref[...] = p[:, 3 * w:4 * w]
    kf_ref[...] = p[:, 4 * w:5 * w]
    vf_ref[...] = p[:, 5 * w:6 * w]
    lane = lax.broadcasted_iota(jnp.int32, logf_ref.shape, 1)
    logf_ref[...] = jnp.where(lane < N_GROUP_HEADS, _log_sigmoid(p[:, 6 * w:6 * w + LANES] + bf_ref[...]), 0.0)


def _inproj_sample(x, shift, scale, g_attn, w_in, b_forget, pos):
    n, d = x.shape
    nh = N_GROUP_HEADS
    wqm, wkm, wvm, wqf, wkf, wvf, wfl = _split_w_in(w_in)
    w = jnp.concatenate([wqm * Q_SCALE, wkm, wvm, wqf * Q_SCALE, wkf, wvf,
                         jnp.pad(wfl, ((0, 0), (0, LANES - nh)))], axis=1).astype(BF16)
    bf = jnp.pad(b_forget, (0, LANES - nh)).reshape(1, LANES)
    tab, _ = _rope_tables(pos)
    full = lambda shape: pl.BlockSpec(shape, lambda i: (0,) * len(shape))
    wide = jax.ShapeDtypeStruct((n, GROUP_W), F32)
    return pl.pallas_call(
        _inproj_sample_kernel,
        grid=(1,),
        in_specs=[full((n, d)), full((n, d)), full((n, d)), full((1, d)), full(w.shape),
                  full((1, LANES)), full(tab.shape)],
        out_specs=[full((n, GROUP_W))] * 6 + [full((n, LANES))],
        out_shape=[wide] * 6 + [jax.ShapeDtypeStruct((n, LANES), F32)],
        compiler_params=_cparams(("arbitrary",)),
        name="inproj_sample",
    )(x, shift, scale, g_attn.reshape(1, d), w, bf, tab)


def _expand_heads(a):
    n = a.shape[0]
    return jnp.concatenate([jnp.broadcast_to(a[hh:hh + 1], (SUBLANES, a.shape[1])) for hh in range(n)],
                           axis=0)


def _dot3r(x, b01):
    hi, mid, lo = _split3(x)
    d = functools.partial(jnp.dot, preferred_element_type=F32)
    return d(hi, b01) + d(mid, b01) + d(lo, b01)


def _logf_suffix_kernel(lf_ref, d_ref, tot_ref):
    page = lf_ref.shape[1]
    kr = lax.broadcasted_iota(jnp.int32, (page, page), 0)
    kc = lax.broadcasted_iota(jnp.int32, (page, page), 1)
    x = lf_ref[...] * LOG2E
    d_ref[...] = _dot3r(x, jnp.where(kr > kc, 1.0, 0.0).astype(BF16))
    tot_ref[...] = _dot3r(x, jnp.ones((page, page), BF16))


def _logf_suffix(clf_t):
    n_pool, nh, page = clf_t.shape
    rows = n_pool * nh
    tr = 2048 if rows % 2048 == 0 else rows
    spec = pl.BlockSpec((tr, page), lambda i: (i, 0))
    shp = jax.ShapeDtypeStruct((rows, page), F32)
    d, tot = pl.pallas_call(
        _logf_suffix_kernel,
        grid=(rows // tr,),
        in_specs=[spec],
        out_specs=[spec, spec],
        out_shape=[shp, shp],
        compiler_params=_cparams(("arbitrary",)),
        name="logf_suffix",
    )(clf_t.reshape(rows, page))
    return d.reshape(n_pool, nh, page), tot.reshape(n_pool, nh, page)


SAMPLE_PAGES_PER_STEP = 4


def _sample_attn_kernel(pt_ref, qm_ref, qf_ref, knm_ref, vnm_ref, knf_ref, vnf_ref, lnew_ref, *refs,
                        n_steps, rows, pages_per_block):
    del pt_ref
    pp = SAMPLE_PAGES_PER_STEP
    page_refs = [refs[6 * p:6 * p + 6] for p in range(pp)]
    om_ref, of_ref = refs[6 * pp:6 * pp + 2]
    mf_ref, lfs_ref, accf_ref, csuf_ref, gates_ref, mstat_ref, lstat_ref, ob_ref = refs[6 * pp + 2:]
    j = pl.program_id(1)
    nh, hd, page = page_refs[0][0].shape
    w = nh * hd
    blocks_per_step = pp // pages_per_block
    nblk = n_steps * blocks_per_step
    lane = lax.broadcasted_iota(jnp.int32, (rows, LANES), 1)
    trow = jnp.bitwise_and(lax.broadcasted_iota(jnp.int32, (rows, LANES), 0), SUBLANES - 1)
    flat = lambda ref: ref[...].reshape(w, page).astype(BF16)

    @pl.when(j == 0)
    def _():
        mf_ref[...] = jnp.full_like(mf_ref, -jnp.inf)
        lfs_ref[...] = jnp.zeros_like(lfs_ref)
        accf_ref[...] = jnp.zeros_like(accf_ref)
        csuf_ref[...] = jnp.zeros_like(csuf_ref)
        gates_ref[...] = jnp.zeros_like(gates_ref)
        mstat_ref[...] = jnp.zeros_like(mstat_ref)
        lstat_ref[...] = jnp.zeros_like(lstat_ref)

    qm = qm_ref[...]
    qm_b = qm.astype(BF16)
    qf_b = qf_ref[...].astype(BF16)
    ones_b = jnp.ones((SUBLANES, page), BF16)
    dot = functools.partial(jnp.dot, preferred_element_type=F32)

    a_new = jnp.where(lane <= trow, lnew_ref[...], 0.0)
    cq = jnp.sum(a_new, axis=1, keepdims=True) * LOG2E

    kts_m = [flat(r[0]) for r in page_refs]
    s_m = [dot(qm_b, kt) for kt in kts_m]
    s_f = [dot(qf_b, flat(r[2])) for r in page_refs]
    ksum = [_nt_dot(ones_b, kt)[0:1] for kt in kts_m]

    suffix = csuf_ref[...]
    for p in range(pp):
        s_f[p] = s_f[p] + _expand_heads(page_refs[p][4][...]) + (suffix + cq)
        suffix = suffix + _expand_heads(page_refs[p][5][...])
    csuf_ref[...] = suffix
    m_old = mf_ref[...]
    m_new = m_old
    for p in range(pp):
        m_new = jnp.maximum(m_new, jnp.max(s_f[p], axis=1, keepdims=True))
    alpha = jnp.exp2(m_old - m_new)
    l_new = alpha * lfs_ref[...]
    acc = alpha * accf_ref[...]
    for p in range(pp):
        pr = jnp.exp2(s_f[p] - m_new)
        l_new = l_new + jnp.sum(pr, axis=1, keepdims=True)
        acc = acc + _nt_dot(pr.astype(BF16), flat(page_refs[p][3]))
    mf_ref[...] = m_new
    lfs_ref[...] = l_new
    accf_ref[...] = acc

    for bb in range(blocks_per_step):
        pages = range(bb * pages_per_block, (bb + 1) * pages_per_block)
        n = nblk - 1 - (j * blocks_per_step + bb)
        m_b = jnp.max(s_m[pages[0]], axis=1, keepdims=True)
        for p in pages[1:]:
            m_b = jnp.maximum(m_b, jnp.max(s_m[p], axis=1, keepdims=True))
        l_b = jnp.zeros((rows, 1), F32)
        o_b = jnp.zeros((rows, w), F32)
        ks = jnp.zeros((1, w), F32)
        for p in pages:
            pr = jnp.exp2(s_m[p] - m_b)
            l_b = l_b + jnp.sum(pr, axis=1, keepdims=True)
            o_b = o_b + _nt_dot(pr.astype(BF16), flat(page_refs[p][1]))
            ks = ks + ksum[p]
        gate = jnp.sum(qm * (ks * (1.0 / (page * pages_per_block))), axis=1, keepdims=True)
        hit = lane == n
        gates_ref[...] = jnp.where(hit, gate, gates_ref[...])
        mstat_ref[...] = jnp.where(hit, m_b, mstat_ref[...])
        lstat_ref[...] = jnp.where(hit, l_b, lstat_ref[...])
        ob_ref[n] = o_b

    @pl.when(j == n_steps - 1)
    def _():
        causal = lane <= trow
        kr = lax.broadcasted_iota(jnp.int32, (page, page), 0)
        kc = lax.broadcasted_iota(jnp.int32, (page, page), 1)
        tri_after = jnp.where(kr > kc, 1.0, 0.0).astype(BF16)
        s_new = dot(qf_b, flat(knf_ref)) + _dot3r(a_new * LOG2E, tri_after)
        s_new = jnp.where(causal, s_new, -jnp.inf)
        m_fin = jnp.maximum(mf_ref[...], jnp.max(s_new, axis=1, keepdims=True))
        a_fin = jnp.exp2(mf_ref[...] - m_fin)
        p_new = jnp.exp2(s_new - m_fin)
        l_fin = a_fin * lfs_ref[...] + jnp.sum(p_new, axis=1, keepdims=True)
        of_ref[...] = (a_fin * accf_ref[...] + _nt_dot(p_new.astype(BF16), flat(vnf_ref))) / l_fin

        s_own = jnp.where(causal, dot(qm_b, flat(knm_ref)), -jnp.inf)
        m_own = jnp.max(s_own, axis=1, keepdims=True)
        p_own = jnp.exp2(s_own - m_own)
        l_own = jnp.sum(p_own, axis=1, keepdims=True)
        o_own = _nt_dot(p_own.astype(BF16), flat(vnm_ref))

        valid = lane < nblk
        g = jnp.where(valid, gates_ref[...], -jnp.inf)
        sel = lane < 0
        for _ in range(MOBA_TOPK):
            mx = jnp.max(g, axis=1, keepdims=True)
            first = jnp.min(jnp.where(g == mx, lane, LANES), axis=1, keepdims=True)
            pick = lane == first
            sel = sel | (pick & valid)
            g = jnp.where(pick, -jnp.inf, g)
        mstat = mstat_ref[...]
        m_all = jnp.maximum(m_own, jnp.max(jnp.where(sel, mstat, -jnp.inf), axis=1, keepdims=True))
        wgt = jnp.where(sel, jnp.exp2(mstat - m_all), 0.0)
        w_own = jnp.exp2(m_own - m_all)
        denom = jnp.sum(wgt * lstat_ref[...], axis=1, keepdims=True) + w_own * l_own

        def body(n, numer):
            wcol = jnp.sum(jnp.where(lane == n, wgt, 0.0), axis=1, keepdims=True)
            return numer + wcol * ob_ref[n]

        numer = lax.fori_loop(0, nblk, body, w_own * o_own)
        om_ref[...] = numer / denom


def _sample_attention(page_table, qbd_m, qbd_f, knew_m, vnew_m, knew_f, vnew_f, lnew,
                      ckt_m, cvt_m, ckt_f, cvt_f, dec, tot):
    nseq, rows, w = qbd_m.shape
    n_pages = page_table.shape[1]
    nh, hd, page = ckt_m.shape[1:]
    pp = SAMPLE_PAGES_PER_STEP
    pages_per_block = MOBA_BLOCK // page
    assert n_pages % pp == 0 and pp % pages_per_block == 0
    n_steps = n_pages // pp
    nblk = n_pages // pages_per_block
    pt_flat = page_table.reshape(-1)

    seq = lambda shape: pl.BlockSpec((None,) + shape, lambda b, j, pt: (b,) + (0,) * len(shape))

    def pg(shape, p):
        return pl.BlockSpec(
            (None,) + shape,
            lambda b, j, pt: (pt[b * n_pages + n_pages - 1 - (j * pp + p)],) + (0,) * len(shape))

    page_specs, page_args = [], []
    for p in range(pp):
        page_specs += [pg((nh, hd, page), p)] * 4 + [pg((nh, page), p)] * 2
        page_args += [ckt_m, cvt_m, ckt_f, cvt_f, dec, tot]
    col = lambda: pltpu.VMEM((rows, 1), F32)
    stat = lambda: pltpu.VMEM((rows, LANES), F32)
    grid_spec = pltpu.PrefetchScalarGridSpec(
        num_scalar_prefetch=1,
        grid=(nseq, n_steps),
        in_specs=[seq((rows, w)), seq((rows, w)), seq((nh, hd, page)), seq((nh, hd, page)),
                  seq((nh, hd, page)), seq((nh, hd, page)), seq((rows, LANES))] + page_specs,
        out_specs=[seq((rows, w)), seq((rows, w))],
        scratch_shapes=[col(), col(), pltpu.VMEM((rows, w), F32), stat(),
                        stat(), stat(), stat(), pltpu.VMEM((nblk, rows, w), F32)])
    return pl.pallas_call(
        functools.partial(_sample_attn_kernel, n_steps=n_steps, rows=rows,
                          pages_per_block=pages_per_block),
        grid_spec=grid_spec,
        out_shape=[jax.ShapeDtypeStruct((nseq, rows, w), F32)] * 2,
        compiler_params=_cparams(("arbitrary", "arbitrary")),
        name="sample_attention",
    )(pt_flat, qbd_m, qbd_f, knew_m, vnew_m, knew_f, vnew_f, lnew, *page_args)


def _block_diag_queries(q, nseq, t):
    nh = N_GROUP_HEADS
    q4 = q.reshape(nseq, t, nh, HEAD_DIM)
    eye = jnp.eye(nh, dtype=q.dtype)
    return jnp.einsum("bthd,hg->bhtgd", q4, eye).reshape(nseq, nh * t, nh * HEAD_DIM)


def _diag_heads(o, nseq, t):
    nh = N_GROUP_HEADS
    o5 = o.reshape(nseq, nh, t, nh, HEAD_DIM)
    eye = jnp.eye(nh, dtype=o.dtype)
    return jnp.einsum("bhtgd,hg->bthd", o5, eye).reshape(nseq * t, nh * HEAD_DIM)


def _sample_mixers(x, mod, g_attn, w_in, b_forget, caches, page_table):
    nseq, t, d = x.shape
    nh = N_GROUP_HEADS
    n = nseq * t
    ck_m, cv_m, ck_f, cv_f, clf = caches
    page = ck_m.shape[1]
    past_len = page_table.shape[1] * page
    pos = jnp.tile(past_len + jnp.arange(t, dtype=jnp.int32), nseq)
    per_tok = lambda a: jnp.broadcast_to(a[:, None, :], (nseq, t, d)).reshape(n, d)
    qm, km, vm, qf, kf, vf, logf = _inproj_sample(
        x.reshape(n, d), per_tok(mod[:, 0]), per_tok(mod[:, 1]), g_attn, w_in, b_forget, pos)
    shp = (nseq, t, nh, HEAD_DIM)
    pages_t = lambda c: jnp.transpose(c, (0, 2, 3, 1))
    new_page = lambda a: jnp.pad(pages_t(a.reshape(shp)), ((0, 0), (0, 0), (0, 0), (0, page - t)))
    lf_new = logf[:, 0:nh].reshape(nseq, t, nh)
    lnew = jnp.pad(jnp.repeat(jnp.swapaxes(lf_new, 1, 2), t, axis=1), ((0, 0), (0, 0), (0, LANES - t)))
    dec, tot = _logf_suffix(jnp.swapaxes(clf, 1, 2))
    om, of = _sample_attention(
        page_table, _block_diag_queries(qm, nseq, t), _block_diag_queries(qf, nseq, t),
        new_page(km), new_page(vm), new_page(kf), new_page(vf), lnew,
        pages_t(ck_m), pages_t(cv_m), pages_t(ck_f), pages_t(cv_f), dec, tot)
    otm = _diag_heads(om, nseq, t).T[None]
    otf = _diag_heads(of, nseq, t).T[None]
    new_rows = (km.reshape(shp), vm.reshape(shp), kf.reshape(shp), vf.reshape(shp), lf_new)
    return otm, otf, new_rows


def _prompt_mixers(x, mod, g_attn, w_in, b_forget):
    b, s, d = x.shape
    (km, vm, kf, vf, logf, kpm, kpf, qtm, qtg, qtf, fqt, vtm, vtf, kmean) = _inproj_prompt(
        x, mod[:, 0:1], mod[:, 1:2], g_attn, w_in, b_forget, ts=MOBA_BLOCK)
    bias = _moba_gate(qtg, kmean, tg=min(s, 2048))
    otm = _attention(qtm, bias, kpm, vtm, MOBA_EXT_ROWS, MOBA_BLOCK)
    otf = _attention(qtf, fqt, kpf, vtf, FOX_EXT_ROWS, MOBA_BLOCK)
    nh = N_GROUP_HEADS
    new_rows = (km.reshape(b, s, nh, HEAD_DIM), vm.reshape(b, s, nh, HEAD_DIM),
                kf.reshape(b, s, nh, HEAD_DIM), vf.reshape(b, s, nh, HEAD_DIM), logf[:, :, 0:nh])
    return otm, otf, new_rows


def _peer_ffn(h2, x1, gate, peer_tables, g_final, tt_route, tt, te, tokens_per_gate):
    wqt, subkeys, u_bf, vt_bf = peer_tables
    route = _peer_route(h2, wqt, subkeys, tt_route)
    return _peer_experts(h2, u_bf, vt_bf, route, x1, gate, g_final, tt, te, tokens_per_gate)


def kernel(x_prompt, x_sample, c_prompt, c_sample, cache_k_moba, cache_v_moba, cache_k_fox, cache_v_fox, cache_logf_fox, page_table, w_ada, b_ada, g_attn, g_ffn, w_in, b_forget, g_out_moba, g_out_fox, w_out, peer_wq, peer_subkeys, peer_u, peer_v, g_final):
    b, s, d = x_prompt.shape
    nseq, t, _ = x_sample.shape
    depth = w_ada.shape[0]
    assert depth == 1 and t == SUBLANES and s % MOBA_BLOCK == 0 and s // MOBA_BLOCK <= MOBA_EXT_ROWS
    l = 0

    c_all = jnp.concatenate([c_prompt, c_sample], axis=0)
    pad = -c_all.shape[0] % SUBLANES
    mod = _modulation(jnp.pad(c_all, ((0, pad), (0, 0))), w_ada[l], b_ada[l])
    mod = mod[:b + nseq].reshape(b + nseq, N_ADA, d)
    mp, ms = mod[:b], mod[b:]

    peer_tables = (peer_wq[l].T.astype(BF16),
                   peer_subkeys[l].reshape(PEER_HEADS * 2, PEER_NKEYS, PEER_HALF),
                   peer_u[l].astype(BF16), peer_v[l].T.astype(BF16))

    otm, otf, rows_p = _prompt_mixers(x_prompt, mp, g_attn[l], w_in[l], b_forget[l])
    x1, h2 = _merge(x_prompt, otm, otf, mp[:, 2:3], mp[:, 3:4], mp[:, 4:5],
                    g_out_moba[l], g_out_fox[l], g_ffn[l], w_out[l], tm=min(s, 512))
    y_prompt = _peer_ffn(h2.reshape(b * s, d), x1.reshape(b * s, d), mp[:, 5:6], peer_tables, g_final,
                         tt_route=2 * LANES, tt=min(s, 512), te=1024, tokens_per_gate=s).reshape(b, s, d)

    n = nseq * t
    per_tok = lambda a: jnp.broadcast_to(a[:, None, :], (nseq, t, d)).reshape(1, n, d)
    caches = (cache_k_moba[l], cache_v_moba[l], cache_k_fox[l], cache_v_fox[l], cache_logf_fox[l])
    otm_s, otf_s, rows_s = _sample_mixers(x_sample, ms, g_attn[l], w_in[l], b_forget[l], caches, page_table)
    x1s, h2s = _merge(x_sample.reshape(1, n, d), otm_s, otf_s, per_tok(ms[:, 2]), per_tok(ms[:, 3]),
                      per_tok(ms[:, 4]), g_out_moba[l], g_out_fox[l], g_ffn[l], w_out[l], tm=n)
    y_sample = _peer_ffn(h2s.reshape(n, d), x1s.reshape(n, d), per_tok(ms[:, 5]), peer_tables, g_final,
                         tt_route=min(n, 2 * LANES), tt=n, te=1024, tokens_per_gate=n).reshape(nseq, t, d)

    stack = lambda a: a[None]
    return (y_prompt, y_sample) + tuple(stack(a) for a in rows_p) + tuple(stack(a) for a in rows_s)
```

```python
import functools
import math

import numpy as np
import jax
import jax.numpy as jnp
from jax import lax
from jax.experimental import pallas as pl
from jax.experimental.pallas import tpu as pltpu

F32 = jnp.float32
BF16 = jnp.bfloat16
HIGHEST = lax.Precision.HIGHEST

HEAD_DIM = 64
N_GROUP_HEADS = 8
GROUP_W = N_GROUP_HEADS * HEAD_DIM
MOBA_BLOCK = 256
MOBA_BLOCK_LOG2 = 8
MOBA_TOPK = 3
ROPE_THETA = 500000.0
ROPE_DIM = HEAD_DIM // 4
ROPE_HALF = ROPE_DIM // 2
PEER_HEADS = 8
PEER_NKEYS = 128
PEER_HALF = 128
PEER_TOPK = 16
N_ADA = 6
EPS = 1e-6

LANES = 128
SUBLANES = 8
BF16_SUBLANES = 16
VMEM_LIMIT_BYTES = 56 * 1024 * 1024

LOG2E = math.log2(math.e)
Q_SCALE = HEAD_DIM ** -0.5 * LOG2E
NEG_BIG = -1.0e30
PAD_W = 2 * HEAD_DIM
FOX_EXT_ROWS = 16
MOBA_EXT_ROWS = 32


def _cparams(sem):
    return pltpu.CompilerParams(dimension_semantics=sem, vmem_limit_bytes=VMEM_LIMIT_BYTES)


def _nt_dot(a, b):
    return lax.dot_general(a, b, (((1,), (1,)), ((), ())), preferred_element_type=F32)


def _split3(x):
    hi = x.astype(BF16)
    r1 = x - hi.astype(F32)
    mid = r1.astype(BF16)
    lo = (r1 - mid.astype(F32)).astype(BF16)
    return hi, mid, lo


def _dot3(a01, x):
    hi, mid, lo = _split3(x)
    d = functools.partial(jnp.dot, preferred_element_type=F32)
    return d(a01, hi) + d(a01, mid) + d(a01, lo)


def _log_sigmoid(z):
    return jnp.minimum(z, 0.0) - jnp.log1p(jnp.exp(-jnp.abs(z)))


def _rms_scale(x):
    return lax.rsqrt(jnp.mean(x * x, axis=-1, keepdims=True) + EPS)


def _mod_kernel(c_ref, w_ref, b_ref, o_ref):
    c = c_ref[...]
    s = c / (1.0 + jnp.exp(-c))
    o_ref[...] = jnp.dot(s, w_ref[...], precision=HIGHEST, preferred_element_type=F32) + b_ref[...]


def _modulation(c_all, w_ada, b_ada):
    m, d = c_all.shape
    n = w_ada.shape[1]
    tn = 1536
    return pl.pallas_call(
        _mod_kernel,
        grid=(n // tn,),
        in_specs=[pl.BlockSpec((m, d), lambda j: (0, 0)),
                  pl.BlockSpec((d, tn), lambda j: (0, j)),
                  pl.BlockSpec((1, tn), lambda j: (0, j))],
        out_specs=pl.BlockSpec((m, tn), lambda j: (0, j)),
        out_shape=jax.ShapeDtypeStruct((m, n), F32),
        compiler_params=_cparams(("arbitrary",)),
        name="modulation",
    )(c_all, w_ada, b_ada.reshape(1, n))


def _rope_tables(pos):
    inv = ROPE_THETA ** (-jnp.arange(ROPE_HALF, dtype=F32) / ROPE_HALF)
    ang = pos.astype(F32)[:, None] * inv[None, :]
    cos, sin = jnp.cos(ang), jnp.sin(ang)
    t = pos.shape[0]
    one = jnp.ones((t, HEAD_DIM - ROPE_DIM), F32)
    zero = jnp.zeros((t, HEAD_DIM - ROPE_DIM), F32)
    z8 = jnp.zeros((t, ROPE_HALF), F32)
    c = jnp.concatenate([cos, cos, one], axis=1)
    sm = jnp.concatenate([-sin, z8, zero], axis=1)
    sp = jnp.concatenate([z8, sin, zero], axis=1)
    nat = jnp.stack([jnp.tile(a, (1, 2)) for a in (c, sm, sp)])
    tr = jnp.stack([cos.T, sin.T])
    return nat, tr


def _rope_nat(x, tab_ref):
    xm = pltpu.roll(x, LANES - ROPE_HALF, axis=1)
    xp = pltpu.roll(x, ROPE_HALF, axis=1)
    return x * tab_ref[0] + xm * tab_ref[1] + xp * tab_ref[2]


def _inproj_kernel(x_ref, shift_ref, scale_ref, g_ref, wnat_ref, wt_ref, bf_ref, tab_ref, tabt_ref,
                   ltri_ref, pk_ref, pq_ref,
                   km_ref, vm_ref, kf_ref, vf_ref, logf_ref, kpm_ref, kpf_ref,
                   qtm_ref, qtg_ref, qtf_ref, fqt_ref, vtm_ref, vtf_ref, kmean_ref,
                   carry_ref, *, ts):
    j = pl.program_id(1)
    nh = N_GROUP_HEADS
    x = x_ref[...]
    h = (x * _rms_scale(x) * g_ref[...]) * (1.0 + scale_ref[...]) + shift_ref[...]
    hb = h.astype(BF16)
    pn = jnp.dot(hb, wnat_ref[...], preferred_element_type=F32)
    pt = _nt_dot(wt_ref[...], hb)
    w_sp = nh * PAD_W
    kpm = pn[:, 0:w_sp]
    kpf = pn[:, w_sp:2 * w_sp]
    vm_ref[...] = pn[:, 2 * w_sp:2 * w_sp + GROUP_W]
    vf_ref[...] = pn[:, 2 * w_sp + GROUP_W:2 * w_sp + 2 * GROUP_W]
    fl = pn[:, 2 * w_sp + 2 * GROUP_W:2 * w_sp + 2 * GROUP_W + LANES]

    lane = lax.broadcasted_iota(jnp.int32, (ts, PAD_W), 1)
    row = lax.broadcasted_iota(jnp.int32, (ts, PAD_W), 0)
    blk = jnp.right_shift(j * ts + row, MOBA_BLOCK_LOG2)
    onehot = lane == HEAD_DIM + blk
    km_heads = [_rope_nat(kpm[:, hh * PAD_W:(hh + 1) * PAD_W], tab_ref) for hh in range(nh)]
    nblk = ts // MOBA_BLOCK
    for hh in range(nh):
        kh = km_heads[hh]
        kpm_ref[:, hh * PAD_W:(hh + 1) * PAD_W] = jnp.where(onehot, 1.0, kh).astype(BF16)
        for bb in range(nblk):
            kmean_ref[bb, :, hh * PAD_W:(hh + 1) * PAD_W] = jnp.mean(
                kh[bb * MOBA_BLOCK:(bb + 1) * MOBA_BLOCK], axis=0, keepdims=True)
    for pp in range(nh // 2):
        km_ref[:, pp * PAD_W:(pp + 1) * PAD_W] = (
            km_heads[2 * pp] + pltpu.roll(km_heads[2 * pp + 1], HEAD_DIM, axis=1))
        kf_ref[:, pp * PAD_W:(pp + 1) * PAD_W] = (
            kpf[:, 2 * pp * PAD_W:(2 * pp + 1) * PAD_W]
            + pltpu.roll(kpf[:, (2 * pp + 1) * PAD_W:(2 * pp + 2) * PAD_W], HEAD_DIM, axis=1))

    lane1 = lax.broadcasted_iota(jnp.int32, (ts, LANES), 1)
    logf = jnp.where(lane1 < nh, _log_sigmoid(fl + bf_ref[...]), 0.0)
    logf_ref[...] = logf

    @pl.when(j == 0)
    def _():
        carry_ref[...] = jnp.zeros_like(carry_ref)

    cum = _dot3(ltri_ref[...], logf) + carry_ref[0:1, :]
    carry_ref[0:1, :] = cum[ts - 1:ts, :]
    g = cum * LOG2E
    g_hi, g_mid, g_lo = _split3(g)
    parts = (g_hi.astype(F32) + pltpu.roll(g_mid.astype(F32), nh, axis=1)
             + pltpu.roll(g_lo.astype(F32), 2 * nh, axis=1)
             + jnp.where(lane1 == 3 * nh, 1.0, 0.0)).astype(BF16)
    kpf_ref[...] = (kpf + jnp.dot(parts, pk_ref[...], preferred_element_type=F32)).astype(BF16)
    fqt_ref[...] = _nt_dot(pq_ref[...], parts).astype(BF16)

    qtm = pt[0:GROUP_W]
    cos_t = tabt_ref[0]
    sin_t = tabt_ref[1]
    pieces = []
    for hh in range(nh):
        base = hh * HEAD_DIM
        x1 = qtm[base:base + ROPE_HALF]
        x2 = qtm[base + ROPE_HALF:base + ROPE_DIM]
        pieces += [x1 * cos_t - x2 * sin_t, x2 * cos_t + x1 * sin_t, qtm[base + ROPE_DIM:base + HEAD_DIM]]
    qtm = jnp.concatenate(pieces, axis=0)
    qtg_ref[...] = qtm
    qtm_ref[...] = qtm.astype(BF16)
    qtf_ref[...] = pt[GROUP_W:2 * GROUP_W].astype(BF16)
    vtm = pt[2 * GROUP_W:3 * GROUP_W].astype(BF16)
    vtf = pt[3 * GROUP_W:4 * GROUP_W].astype(BF16)
    for bb in range(nblk):
        sl = slice(bb * MOBA_BLOCK, (bb + 1) * MOBA_BLOCK)
        for hh in range(nh):
            vtm_ref[hh, bb] = vtm[hh * HEAD_DIM:(hh + 1) * HEAD_DIM, sl]
            vtf_ref[hh, bb] = vtf[hh * HEAD_DIM:(hh + 1) * HEAD_DIM, sl]


def _spread_heads(w):
    d = w.shape[0]
    w = w.reshape(d, N_GROUP_HEADS, HEAD_DIM)
    return jnp.pad(w, ((0, 0), (0, 0), (0, PAD_W - HEAD_DIM))).reshape(d, N_GROUP_HEADS * PAD_W)


def _split_w_in(w_in):
    cuts = np.cumsum([GROUP_W] * 6).tolist()
    return jnp.split(w_in, cuts, axis=1)


def _fox_placement():
    nh = N_GROUP_HEADS
    pk = np.zeros((LANES, nh * PAD_W), np.float32)
    pq = np.zeros((nh * FOX_EXT_ROWS, LANES), np.float32)
    for hh in range(nh):
        for t in range(3):
            pk[t * nh + hh, hh * PAD_W + HEAD_DIM + t] = -1.0
            pk[3 * nh, hh * PAD_W + HEAD_DIM + 3 + t] = 1.0
            pq[hh * FOX_EXT_ROWS + t, 3 * nh] = 1.0
            pq[hh * FOX_EXT_ROWS + 3 + t, t * nh + hh] = 1.0
    return jnp.asarray(pk, BF16), jnp.asarray(pq, BF16)


def _inproj_prompt(x, shift, scale, g_attn, w_in, b_forget, ts):
    b, s, d = x.shape
    nh = N_GROUP_HEADS
    nb = s // MOBA_BLOCK
    wqm, wkm, wvm, wqf, wkf, wvf, wfl = _split_w_in(w_in)
    wnat = jnp.concatenate([_spread_heads(wkm), _spread_heads(wkf), wvm, wvf,
                            jnp.pad(wfl, ((0, 0), (0, LANES - nh)))], axis=1).astype(BF16)
    wt = jnp.concatenate([wqm * Q_SCALE, wqf * Q_SCALE, wvm, wvf], axis=1).T.astype(BF16)
    bf = jnp.pad(b_forget, (0, LANES - nh)).reshape(1, LANES)
    tab, tabt = _rope_tables(jnp.arange(s, dtype=jnp.int32))
    ltri = jnp.tril(jnp.ones((ts, ts), F32)).astype(BF16)
    pk, pq = _fox_placement()
    nblk = ts // MOBA_BLOCK
    w_sp = nh * PAD_W

    def tok(width, dtype):
        return (jax.ShapeDtypeStruct((b, s, width), dtype),
                pl.BlockSpec((None, ts, width), lambda i, j: (i, j, 0)))

    def tr(rows, dtype):
        return (jax.ShapeDtypeStruct((b, rows, s), dtype),
                pl.BlockSpec((None, rows, ts), lambda i, j: (i, 0, j)))

    vt = (jax.ShapeDtypeStruct((b, nh, nb, HEAD_DIM, MOBA_BLOCK), BF16),
          pl.BlockSpec((None, nh, nblk, HEAD_DIM, MOBA_BLOCK), lambda i, j: (i, 0, j, 0, 0)))
    kmean = (jax.ShapeDtypeStruct((b, nb, 1, w_sp), F32),
             pl.BlockSpec((None, nblk, 1, w_sp), lambda i, j: (i, j, 0, 0)))
    outs = [tok(GROUP_W, F32), tok(GROUP_W, F32), tok(GROUP_W, F32), tok(GROUP_W, F32),
            tok(LANES, F32), tok(w_sp, BF16), tok(w_sp, BF16),
            tr(GROUP_W, BF16), tr(GROUP_W, F32), tr(GROUP_W, BF16), tr(nh * FOX_EXT_ROWS, BF16),
            vt, vt, kmean]
    const = lambda shape: pl.BlockSpec(shape, lambda i, j: (0,) * len(shape))
    row_spec = pl.BlockSpec((None, 1, d), lambda i, j: (i, 0, 0))
    return pl.pallas_call(
        functools.partial(_inproj_kernel, ts=ts),
        grid=(b, s // ts),
        in_specs=[pl.BlockSpec((None, ts, d), lambda i, j: (i, j, 0)), row_spec, row_spec,
                  const((1, d)), const(wnat.shape), const(wt.shape), const((1, LANES)),
                  pl.BlockSpec((3, ts, LANES), lambda i, j: (0, j, 0)),
                  pl.BlockSpec((2, ROPE_HALF, ts), lambda i, j: (0, 0, j)),
                  const((ts, ts)), const(pk.shape), const(pq.shape)],
        out_specs=[o[1] for o in outs],
        out_shape=[o[0] for o in outs],
        scratch_shapes=[pltpu.VMEM((SUBLANES, LANES), F32)],
        compiler_params=_cparams(("arbitrary", "arbitrary")),
        name="inproj_prompt",
    )(x, shift, scale, g_attn.reshape(1, d), wnat, wt, bf, tab, tabt, ltri, pk, pq)


def _first_max(g, idx, big):
    mx = jnp.max(g, axis=0, keepdims=True)
    first = jnp.min(jnp.where(g == mx, idx, big), axis=0, keepdims=True)
    return mx, first


def _moba_gate_kernel(qt_ref, kmean_ref, bias_ref, *, tg, nb):
    i = pl.program_id(2)
    gate = jnp.dot(kmean_ref[...][:, 0:HEAD_DIM], qt_ref[...], precision=HIGHEST,
                   preferred_element_type=F32)
    n = lax.broadcasted_iota(jnp.int32, (nb, tg), 0)
    qblk = jnp.right_shift(i * tg + lax.broadcasted_iota(jnp.int32, (nb, tg), 1), MOBA_BLOCK_LOG2)
    past = n < qblk
    g = jnp.where(past, gate, -jnp.inf)
    sel = n == qblk
    for _ in range(MOBA_TOPK):
        _, first = _first_max(g, n, nb)
        pick = n == first
        sel = sel | (pick & past)
        g = jnp.where(pick, -jnp.inf, g)
    bias_ref[...] = jnp.where(sel, 0.0, NEG_BIG).astype(BF16)


def _moba_gate(qtg, kmean, tg):
    b, _, s = qtg.shape
    nb = MOBA_EXT_ROWS
    kmean = kmean.reshape(b, kmean.shape[1], N_GROUP_HEADS * PAD_W)
    kmean = jnp.pad(kmean, ((0, 0), (0, nb - kmean.shape[1]), (0, 0)))
    return pl.pallas_call(
        functools.partial(_moba_gate_kernel, tg=tg, nb=nb),
        grid=(b, N_GROUP_HEADS, s // tg),
        in_specs=[pl.BlockSpec((None, HEAD_DIM, tg), lambda bi, h, i: (bi, h, i)),
                  pl.BlockSpec((None, nb, PAD_W), lambda bi, h, i: (bi, 0, h))],
        out_specs=pl.BlockSpec((None, None, nb, tg), lambda bi, h, i: (bi, h, 0, i)),
        out_shape=jax.ShapeDtypeStruct((b, N_GROUP_HEADS, nb, s), BF16),
        compiler_params=_cparams(("arbitrary", "arbitrary", "arbitrary")),
        name="moba_gate",
    )(qtg, kmean)


ATTN_HEADS_PER_STEP = 4


def _attn_kernel(qt_ref, ext_ref, kp_ref, vt_ref, o_ref, qp_ref, *, tq, ext_rows, hb):
    i = pl.program_id(2)
    ext = ext_ref[...].reshape(hb * ext_rows, tq)
    for hh in range(hb):
        qp_ref[hh, 0:HEAD_DIM, :] = qt_ref[hh * HEAD_DIM:(hh + 1) * HEAD_DIM, :]
        qp_ref[hh, HEAD_DIM:HEAD_DIM + ext_rows, :] = ext[hh * ext_rows:(hh + 1) * ext_rows]
        qp_ref[hh, HEAD_DIM + ext_rows:PAD_W, :] = jnp.zeros((PAD_W - HEAD_DIM - ext_rows, tq), BF16)
    qps = [qp_ref[hh] for hh in range(hb)]

    def step(tiles, carry):
        s_all = [[jnp.dot(k[:, hh * PAD_W:(hh + 1) * PAD_W], qps[hh], preferred_element_type=F32)
                  for hh in range(hb)] for k, _, _ in tiles]
        out = []
        for hh in range(hb):
            m, l, acc = carry[3 * hh:3 * hh + 3]
            s_h = [s_all[t][hh] if mask is None else jnp.where(mask, s_all[t][hh], -jnp.inf)
                   for t, (_, _, mask) in enumerate(tiles)]
            m_new = m
            for s_t in s_h:
                m_new = jnp.maximum(m_new, jnp.max(s_t, axis=0, keepdims=True))
            alpha = jnp.exp2(m - m_new)
            l = alpha * l
            acc = alpha * acc
            for s_t, (_, vidx, _) in zip(s_h, tiles):
                p = jnp.exp2(s_t - m_new)
                l = l + jnp.sum(p, axis=0, keepdims=True)
                acc = acc + jnp.dot(vt_ref[hh, vidx], p.astype(BF16), preferred_element_type=F32)
            out += [m_new, l, acc]
        return tuple(out)

    def keys(n):
        return kp_ref[pl.ds(pl.multiple_of(n * tq, tq), tq), :]

    kpos = lax.broadcasted_iota(jnp.int32, (tq, tq), 0)
    qpos = lax.broadcasted_iota(jnp.int32, (tq, tq), 1)
    odd = i % 2
    init = []
    for hh in range(hb):
        init += [jnp.full((1, tq), -jnp.inf, F32), jnp.zeros((1, tq), F32), jnp.zeros((HEAD_DIM, tq), F32)]
    carry = step([(keys(i), i, kpos <= qpos), (keys(0), 0, kpos < odd * tq)], tuple(init))

    def body(u, carry):
        n = odd + 2 * u
        return step([(keys(n), n, None), (keys(n + 1), n + 1, None)], carry)

    fin = lax.fori_loop(0, i // 2, body, carry)
    for hh in range(hb):
        o_ref[hh * HEAD_DIM:(hh + 1) * HEAD_DIM, :] = fin[3 * hh + 2] / fin[3 * hh + 1]


def _attention(qt, ext, kp, vt, ext_rows, tq):
    b, _, s = qt.shape
    hb = ATTN_HEADS_PER_STEP
    ng = N_GROUP_HEADS // hb
    nb = s // tq
    if ext.ndim == 4:
        ext_spec = pl.BlockSpec((None, hb, ext_rows, tq), lambda bi, g, i: (bi, g, 0, i))
    else:
        ext_spec = pl.BlockSpec((None, hb * ext_rows, tq), lambda bi, g, i: (bi, g, i))
    return pl.pallas_call(
        functools.partial(_attn_kernel, tq=tq, ext_rows=ext_rows, hb=hb),
        grid=(b, ng, nb),
        in_specs=[pl.BlockSpec((None, hb * HEAD_DIM, tq), lambda bi, g, i: (bi, g, i)),
                  ext_spec,
                  pl.BlockSpec((None, s, hb * PAD_W), lambda bi, g, i: (bi, 0, g)),
                  pl.BlockSpec((None, hb, nb, HEAD_DIM, tq), lambda bi, g, i: (bi, g, 0, 0, 0))],
        out_specs=pl.BlockSpec((None, hb * HEAD_DIM, tq), lambda bi, g, i: (bi, g, i)),
        out_shape=jax.ShapeDtypeStruct((b, GROUP_W, s), F32),
        scratch_shapes=[pltpu.VMEM((hb, PAD_W, tq), BF16)],
        compiler_params=_cparams(("arbitrary", "arbitrary", "arbitrary")),
        name="attention",
    )(qt, ext, kp, vt)


def _merge_kernel(x_ref, otm_ref, otf_ref, gate_ref, shift_ref, scale_ref, gm_ref, gf_ref, gffn_ref,
                  wout_ref, x1_ref, h2_ref):
    def group_norm(ot, gcol):
        r = lax.rsqrt(jnp.mean(ot * ot, axis=0, keepdims=True) + EPS)
        return (ot * r * gcol).T

    on = jnp.concatenate([group_norm(otm_ref[...], gm_ref[...]),
                          group_norm(otf_ref[...], gf_ref[...])], axis=1).astype(BF16)
    y = jnp.dot(on, wout_ref[...], preferred_element_type=F32)
    x1 = x_ref[...] + gate_ref[...] * y
    x1_ref[...] = x1
    h2 = (x1 * _rms_scale(x1) * gffn_ref[...]) * (1.0 + scale_ref[...]) + shift_ref[...]
    h2_ref[...] = h2.astype(BF16)


def _mod_spec(arr, tm, d):
    if arr.shape[1] == 1:
        return pl.BlockSpec((None, 1, d), lambda i, j: (i, 0, 0))
    return pl.BlockSpec((None, tm, d), lambda i, j: (i, j, 0))


def _merge(x, otm, otf, gate, shift, scale, g_m, g_f, g_ffn, w_out, tm):
    b, s, d = x.shape
    tokspec = pl.BlockSpec((None, tm, d), lambda i, j: (i, j, 0))
    otspec = pl.BlockSpec((None, GROUP_W, tm), lambda i, j: (i, 0, j))
    const = lambda shape: pl.BlockSpec(shape, lambda i, j: (0,) * len(shape))
    return pl.pallas_call(
        _merge_kernel,
        grid=(b, s // tm),
        in_specs=[tokspec, otspec, otspec, _mod_spec(gate, tm, d), _mod_spec(shift, tm, d),
                  _mod_spec(scale, tm, d), const((GROUP_W, 1)), const((GROUP_W, 1)), const((1, d)),
                  const((2 * GROUP_W, d))],
        out_specs=[tokspec, tokspec],
        out_shape=[jax.ShapeDtypeStruct((b, s, d), F32), jax.ShapeDtypeStruct((b, s, d), BF16)],
        compiler_params=_cparams(("arbitrary", "arbitrary")),
        name="merge_outproj",
    )(x, otm, otf, gate, shift, scale, g_m.reshape(GROUP_W, 1), g_f.reshape(GROUP_W, 1),
      g_ffn.reshape(1, d), w_out.astype(BF16))


_CAND_GROUPS = [(0, 16), (1, 8), (2, 5), (3, 4), (4, 3), (5, 2), (6, 2), (7, 2)]
_CAND_ROWS = 16 + 7 * 8 + 8


def _top16(s, tt):
    idx = lax.broadcasted_iota(jnp.int32, (PEER_NKEYS, tt), 0)
    r16 = lax.broadcasted_iota(jnp.int32, (PEER_TOPK, tt), 0)
    g = s
    rank = jnp.full((PEER_NKEYS, tt), float(PEER_TOPK), F32)
    sv = jnp.zeros((PEER_TOPK, tt), F32)
    for r in range(PEER_TOPK):
        mx, first = _first_max(g, idx, PEER_NKEYS)
        pick = idx == first
        rank = jnp.where(pick, float(r), rank)
        g = jnp.where(pick, -jnp.inf, g)
        sv = jnp.where(r16 == r, mx, sv)
    return sv, rank


def _cand_rows(sv0, sv1, tt):
    groups = []
    for a, cnt in _CAND_GROUPS:
        rows = 16 if a == 0 else SUBLANES
        blk = sv0[a:a + 1] + sv1[0:rows]
        if cnt < rows:
            blk = jnp.where(lax.broadcasted_iota(jnp.int32, (rows, tt), 0) < cnt, blk, -jnp.inf)
        groups.append(blk)
    groups.append(sv0[8:16] + sv1[0:1])
    return jnp.concatenate(groups, axis=0)


def _cand_counts(taken):
    sums = [jnp.sum(taken[0:16], axis=0, keepdims=True)]
    sums += [jnp.sum(taken[16 + 8 * g:24 + 8 * g], axis=0, keepdims=True) for g in range(7)]
    return jnp.concatenate(sums + [taken[72:80]], axis=0)


def _cand_top16(cand, top, tt):
    crow = lax.broadcasted_iota(jnp.int32, (_CAND_ROWS, tt), 0)
    g = cand
    z = jnp.zeros((1, tt), F32)
    for _ in range(PEER_TOPK):
        mx, first = _first_max(g, crow, _CAND_ROWS)
        g = jnp.where(crow == first, -jnp.inf, g)
        z = z + jnp.exp(mx - top)
    taken = jnp.where((g == -jnp.inf) & (cand > -jnp.inf), 1.0, 0.0)
    return z, _cand_counts(taken)


def _route_head(s0, s1, tt):
    sv0, rank0 = _top16(s0, tt)
    sv1, rank1 = _top16(s1, tt)
    top = sv0[0:1] + sv1[0:1]
    z, nbc = _cand_top16(_cand_rows(sv0, sv1, tt), top, tt)
    lim0 = jnp.zeros((PEER_NKEYS, tt), F32)
    for a in range(PEER_TOPK):
        lim0 = jnp.where(rank0 == float(a), nbc[a:a + 1], lim0)
    return rank1, jnp.exp(s1 - sv1[0:1]), lim0, jnp.exp(s0 - sv0[0:1]) / z


def _peer_route_kernel(h_ref, wqt_ref, subhi_ref, sublo_ref, rank1_ref, e1_ref, lim0_ref, e0_ref, *, tt):
    qt = _nt_dot(wqt_ref[...], h_ref[...])
    q_hi = qt.astype(BF16)
    q_lo = (qt - q_hi.astype(F32)).astype(BF16)
    d = functools.partial(jnp.dot, preferred_element_type=F32)
    lt = min(tt, LANES)
    for hh in range(PEER_HEADS):
        s = []
        for p in range(2):
            rows = slice((hh * 2 + p) * PEER_HALF, (hh * 2 + p + 1) * PEER_HALF)
            s_hi, s_lo = subhi_ref[hh * 2 + p], sublo_ref[hh * 2 + p]
            s.append(d(s_hi, q_hi[rows]) + (d(s_hi, q_lo[rows]) + d(s_lo, q_hi[rows])))
        for c in range(tt // lt):
            cols = slice(c * lt, (c + 1) * lt)
            rank1, e1, lim0, e0 = _route_head(s[0][:, cols], s[1][:, cols], lt)
            rank1_ref[hh, :, cols] = rank1.astype(BF16)
            e1_ref[hh, :, cols] = e1.astype(BF16)
            lim0_ref[hh, :, cols] = lim0
            e0_ref[hh, :, cols] = e0


def _peer_route(h2, wqt, subkeys, tt):
    n, d = h2.shape
    sub_hi = subkeys.astype(BF16)
    sub_lo = (subkeys - sub_hi.astype(F32)).astype(BF16)
    spec = pl.BlockSpec((PEER_HEADS, PEER_NKEYS, tt), lambda t: (0, 0, t))
    shp = lambda dt: jax.ShapeDtypeStruct((PEER_HEADS, PEER_NKEYS, n), dt)
    return pl.pallas_call(
        functools.partial(_peer_route_kernel, tt=tt),
        grid=(n // tt,),
        in_specs=[pl.BlockSpec((tt, d), lambda t: (t, 0)),
                  pl.BlockSpec(wqt.shape, lambda t: (0, 0)),
                  pl.BlockSpec(subkeys.shape, lambda t: (0, 0, 0)),
                  pl.BlockSpec(subkeys.shape, lambda t: (0, 0, 0))],
        out_specs=[spec] * 4,
        out_shape=[shp(BF16), shp(BF16), shp(F32), shp(F32)],
        compiler_params=_cparams(("arbitrary",)),
        name="peer_route",
    )(h2, wqt, sub_hi, sub_lo)


def _gelu(x):
    return 0.5 * x * (1.0 + lax.erf(x * (2.0 ** -0.5)))


def _peer_expert_kernel(h_ref, u_ref, vt_ref, rank1_ref, e1_ref, lim0_ref, e0_ref, x1_ref, gate_ref,
                        gfin_ref, y_ref, acc_ref, w_ref, *, tt, te):
    e = pl.program_id(1)

    @pl.when(e == 0)
    def _():
        acc_ref[...] = jnp.zeros_like(acc_ref)

    for ii in range(te // PEER_NKEYS):
        i = e * (te // PEER_NKEYS) + ii
        rep = lambda ref, hh: jnp.broadcast_to(ref[hh, pl.ds(i, 1), :], (BF16_SUBLANES, tt)).astype(BF16)
        w = [jnp.zeros((BF16_SUBLANES, tt), BF16) for _ in range(PEER_NKEYS // BF16_SUBLANES)]
        for hh in range(PEER_HEADS):
            lim = rep(lim0_ref, hh)
            e0 = rep(e0_ref, hh)
            for jc in range(len(w)):
                rows = slice(jc * BF16_SUBLANES, (jc + 1) * BF16_SUBLANES)
                w[jc] = w[jc] + jnp.where(rank1_ref[hh, rows, :] < lim, e1_ref[hh, rows, :] * e0,
                                          jnp.zeros((), BF16))
        w_ref[ii * PEER_NKEYS:(ii + 1) * PEER_NKEYS, :] = jnp.concatenate(w, axis=0)
    a_t = _nt_dot(u_ref[...], h_ref[...])
    g = w_ref[...] * _gelu(a_t).astype(BF16)
    acc_ref[...] += jnp.dot(vt_ref[...], g, preferred_element_type=F32)

    @pl.when(e == pl.num_programs(1) - 1)
    def _():
        x2 = x1_ref[...] + gate_ref[...] * acc_ref[...].T
        y_ref[...] = x2 * _rms_scale(x2) * gfin_ref[...]


def _peer_experts(h2, u_bf, vt_bf, route, x1, gate, g_final, tt, te, tokens_per_gate):
    n, d = h2.shape
    ne = u_bf.shape[0]
    rspec = pl.BlockSpec((PEER_HEADS, PEER_NKEYS, tt), lambda t, e: (0, 0, t))
    tokspec = pl.BlockSpec((tt, d), lambda t, e: (t, 0))
    if gate.shape[1] == 1:
        gspec = pl.BlockSpec((None, 1, d), lambda t, e: (t * tt // tokens_per_gate, 0, 0))
    else:
        gspec = pl.BlockSpec((None, tt, d), lambda t, e: (0, t, 0))
    return pl.pallas_call(
        functools.partial(_peer_expert_kernel, tt=tt, te=te),
        grid=(n // tt, ne // te),
        in_specs=[tokspec,
                  pl.BlockSpec((te, d), lambda t, e: (e, 0)),
                  pl.BlockSpec((d, te), lambda t, e: (0, e)),
                  rspec, rspec, rspec, rspec, tokspec, gspec,
                  pl.BlockSpec((1, d), lambda t, e: (0, 0))],
        out_specs=tokspec,
        out_shape=jax.ShapeDtypeStruct((n, d), F32),
        scratch_shapes=[pltpu.VMEM((d, tt), F32), pltpu.VMEM((te, tt), BF16)],
        compiler_params=_cparams(("arbitrary", "arbitrary")),
        name="peer_experts",
    )(h2, u_bf, vt_bf, *route, x1, gate, g_final.reshape(1, d))


def _inproj_sample_kernel(x_ref, shift_ref, scale_ref, g_ref, w_ref, bf_ref, tab_ref,
                          qm_ref, km_ref, vm_ref, qf_ref, kf_ref, vf_ref, logf_ref):
    x = x_ref[...]
    h = (x * _rms_scale(x) * g_ref[...]) * (1.0 + scale_ref[...]) + shift_ref[...]
    p = jnp.dot(h.astype(BF16), w_ref[...], preferred_element_type=F32)
    w = GROUP_W
    for pp in range(w // LANES):
        sl = slice(pp * LANES, (pp + 1) * LANES)
        qm_ref[:, sl] = _rope_nat(p[:, sl], tab_ref)
        km_ref[:, sl] = _rope_nat(p[:, w + pp * LANES:w + (pp + 1) * LANES], tab_ref)
    vm_ref[...] = p[:, 2 * w:3 * w]
    qf_ref[...] = p[:, 3 * w:4 * w]
    kf_ref[...] = p[:, 4 * w:5 * w]
    vf_ref[...] = p[:, 5 * w:6 * w]
    lane = lax.broadcasted_iota(jnp.int32, logf_ref.shape, 1)
    logf_ref[...] = jnp.where(lane < N_GROUP_HEADS, _log_sigmoid(p[:, 6 * w:6 * w + LANES] + bf_ref[...]), 0.0)


def _inproj_sample(x, shift, scale, g_attn, w_in, b_forget, pos):
    n, d = x.shape
    nh = N_GROUP_HEADS
    wqm, wkm, wvm, wqf, wkf, wvf, wfl = _split_w_in(w_in)
    w = jnp.concatenate([wqm * Q_SCALE, wkm, wvm, wqf * Q_SCALE, wkf, wvf,
                         jnp.pad(wfl, ((0, 0), (0, LANES - nh)))], axis=1).astype(BF16)
    bf = jnp.pad(b_forget, (0, LANES - nh)).reshape(1, LANES)
    tab, _ = _rope_tables(pos)
    full = lambda shape: pl.BlockSpec(shape, lambda i: (0,) * len(shape))
    wide = jax.ShapeDtypeStruct((n, GROUP_W), F32)
    return pl.pallas_call(
        _inproj_sample_kernel,
        grid=(1,),
        in_specs=[full((n, d)), full((n, d)), full((n, d)), full((1, d)), full(w.shape),
                  full((1, LANES)), full(tab.shape)],
        out_specs=[full((n, GROUP_W))] * 6 + [full((n, LANES))],
        out_shape=[wide] * 6 + [jax.ShapeDtypeStruct((n, LANES), F32)],
        compiler_params=_cparams(("arbitrary",)),
        name="inproj_sample",
    )(x, shift, scale, g_attn.reshape(1, d), w, bf, tab)


def _expand_heads(a):
    n = a.shape[0]
    return jnp.concatenate([jnp.broadcast_to(a[hh:hh + 1], (SUBLANES, a.shape[1])) for hh in range(n)],
                           axis=0)


def _dot3r(x, b01):
    hi, mid, lo = _split3(x)
    d = functools.partial(jnp.dot, preferred_element_type=F32)
    return d(hi, b01) + d(mid, b01) + d(lo, b01)


def _logf_suffix_kernel(lf_ref, d_ref, tot_ref):
    page = lf_ref.shape[1]
    kr = lax.broadcasted_iota(jnp.int32, (page, page), 0)
    kc = lax.broadcasted_iota(jnp.int32, (page, page), 1)
    x = lf_ref[...] * LOG2E
    d_ref[...] = _dot3r(x, jnp.where(kr > kc, 1.0, 0.0).astype(BF16))
    tot_ref[...] = _dot3r(x, jnp.ones((page, page), BF16))


def _logf_suffix(clf_t):
    n_pool, nh, page = clf_t.shape
    rows = n_pool * nh
    tr = 2048 if rows % 2048 == 0 else rows
    spec = pl.BlockSpec((tr, page), lambda i: (i, 0))
    shp = jax.ShapeDtypeStruct((rows, page), F32)
    d, tot = pl.pallas_call(
        _logf_suffix_kernel,
        grid=(rows // tr,),
        in_specs=[spec],
        out_specs=[spec, spec],
        out_shape=[shp, shp],
        compiler_params=_cparams(("arbitrary",)),
        name="logf_suffix",
    )(clf_t.reshape(rows, page))
    return d.reshape(n_pool, nh, page), tot.reshape(n_pool, nh, page)


SAMPLE_PAGES_PER_STEP = 8


def _sample_attn_kernel(pt_ref, qm_ref, qf_ref, knm_ref, vnm_ref, knf_ref, vnf_ref, lnew_ref, *refs,
                        n_steps, rows, pages_per_block):
    del pt_ref
    pp = SAMPLE_PAGES_PER_STEP
    page_refs = [refs[6 * p:6 * p + 6] for p in range(pp)]
    om_ref, of_ref = refs[6 * pp:6 * pp + 2]
    mf_ref, lfs_ref, accf_ref, csuf_ref, gates_ref, mstat_ref, lstat_ref, ob_ref = refs[6 * pp + 2:]
    j = pl.program_id(1)
    nh, hd, page = page_refs[0][0].shape
    w = nh * hd
    blocks_per_step = pp // pages_per_block
    nblk = n_steps * blocks_per_step
    lane = lax.broadcasted_iota(jnp.int32, (rows, LANES), 1)
    trow = jnp.bitwise_and(lax.broadcasted_iota(jnp.int32, (rows, LANES), 0), SUBLANES - 1)
    flat = lambda ref: ref[...].reshape(w, page).astype(BF16)

    @pl.when(j == 0)
    def _():
        mf_ref[...] = jnp.full_like(mf_ref, -jnp.inf)
        lfs_ref[...] = jnp.zeros_like(lfs_ref)
        accf_ref[...] = jnp.zeros_like(accf_ref)
        csuf_ref[...] = jnp.zeros_like(csuf_ref)
        gates_ref[...] = jnp.zeros_like(gates_ref)
        mstat_ref[...] = jnp.zeros_like(mstat_ref)
        lstat_ref[...] = jnp.zeros_like(lstat_ref)

    qm = qm_ref[...]
    qm_b = qm.astype(BF16)
    qf_b = qf_ref[...].astype(BF16)
    ones_b = jnp.ones((SUBLANES, page), BF16)
    dot = functools.partial(jnp.dot, preferred_element_type=F32)

    a_new = jnp.where(lane <= trow, lnew_ref[...], 0.0)
    cq = jnp.sum(a_new, axis=1, keepdims=True) * LOG2E

    kts_m = [flat(r[0]) for r in page_refs]
    s_m = [dot(qm_b, kt) for kt in kts_m]
    s_f = [dot(qf_b, flat(r[2])) for r in page_refs]
    ksum = [_nt_dot(ones_b, kt)[0:1] for kt in kts_m]

    suffix = csuf_ref[...]
    for p in range(pp):
        s_f[p] = s_f[p] + _expand_heads(page_refs[p][4][...]) + (suffix + cq)
        suffix = suffix + _expand_heads(page_refs[p][5][...])
    csuf_ref[...] = suffix
    m_old = mf_ref[...]
    m_new = m_old
    for p in range(pp):
        m_new = jnp.maximum(m_new, jnp.max(s_f[p], axis=1, keepdims=True))
    alpha = jnp.exp2(m_old - m_new)
    l_new = alpha * lfs_ref[...]
    acc = alpha * accf_ref[...]
    for p in range(pp):
        pr = jnp.exp2(s_f[p] - m_new)
        l_new = l_new + jnp.sum(pr, axis=1, keepdims=True)
        acc = acc + _nt_dot(pr.astype(BF16), flat(page_refs[p][3]))
    mf_ref[...] = m_new
    lfs_ref[...] = l_new
    accf_ref[...] = acc

    for bb in range(blocks_per_step):
        pages = range(bb * pages_per_block, (bb + 1) * pages_per_block)
        n = nblk - 1 - (j * blocks_per_step + bb)
        m_b = jnp.max(s_m[pages[0]], axis=1, keepdims=True)
        for p in pages[1:]:
            m_b = jnp.maximum(m_b, jnp.max(s_m[p], axis=1, keepdims=True))
        l_b = jnp.zeros((rows, 1), F32)
        o_b = jnp.zeros((rows, w), F32)
        ks = jnp.zeros((1, w), F32)
        for p in pages:
            pr = jnp.exp2(s_m[p] - m_b)
            l_b = l_b + jnp.sum(pr, axis=1, keepdims=True)
            o_b = o_b + _nt_dot(pr.astype(BF16), flat(page_refs[p][1]))
            ks = ks + ksum[p]
        gate = jnp.sum(qm * (ks * (1.0 / (page * pages_per_block))), axis=1, keepdims=True)
        hit = lane == n
        gates_ref[...] = jnp.where(hit, gate, gates_ref[...])
        mstat_ref[...] = jnp.where(hit, m_b, mstat_ref[...])
        lstat_ref[...] = jnp.where(hit, l_b, lstat_ref[...])
        ob_ref[n] = o_b

    @pl.when(j == n_steps - 1)
    def _():
        causal = lane <= trow
        kr = lax.broadcasted_iota(jnp.int32, (page, page), 0)
        kc = lax.broadcasted_iota(jnp.int32, (page, page), 1)
        tri_after = jnp.where(kr > kc, 1.0, 0.0).astype(BF16)
        s_new = dot(qf_b, flat(knf_ref)) + _dot3r(a_new * LOG2E, tri_after)
        s_new = jnp.where(causal, s_new, -jnp.inf)
        m_fin = jnp.maximum(mf_ref[...], jnp.max(s_new, axis=1, keepdims=True))
        a_fin = jnp.exp2(mf_ref[...] - m_fin)
        p_new = jnp.exp2(s_new - m_fin)
        l_fin = a_fin * lfs_ref[...] + jnp.sum(p_new, axis=1, keepdims=True)
        of_ref[...] = (a_fin * accf_ref[...] + _nt_dot(p_new.astype(BF16), flat(vnf_ref))) / l_fin

        s_own = jnp.where(causal, dot(qm_b, flat(knm_ref)), -jnp.inf)
        m_own = jnp.max(s_own, axis=1, keepdims=True)
        p_own = jnp.exp2(s_own - m_own)
        l_own = jnp.sum(p_own, axis=1, keepdims=True)
        o_own = _nt_dot(p_own.astype(BF16), flat(vnm_ref))

        valid = lane < nblk
        g = jnp.where(valid, gates_ref[...], -jnp.inf)
        sel = lane < 0
        for _ in range(MOBA_TOPK):
            mx = jnp.max(g, axis=1, keepdims=True)
            first = jnp.min(jnp.where(g == mx, lane, LANES), axis=1, keepdims=True)
            pick = lane == first
            sel = sel | (pick & valid)
            g = jnp.where(pick, -jnp.inf, g)
        mstat = mstat_ref[...]
        m_all = jnp.maximum(m_own, jnp.max(jnp.where(sel, mstat, -jnp.inf), axis=1, keepdims=True))
        wgt = jnp.where(sel, jnp.exp2(mstat - m_all), 0.0)
        w_own = jnp.exp2(m_own - m_all)
        denom = jnp.sum(wgt * lstat_ref[...], axis=1, keepdims=True) + w_own * l_own

        def body(n, numer):
            wcol = jnp.sum(jnp.where(lane == n, wgt, 0.0), axis=1, keepdims=True)
            return numer + wcol * ob_ref[n]

        numer = lax.fori_loop(0, nblk, body, w_own * o_own)
        om_ref[...] = numer / denom


def _sample_attention(page_table, qbd_m, qbd_f, knew_m, vnew_m, knew_f, vnew_f, lnew,
                      ckt_m, cvt_m, ckt_f, cvt_f, dec, tot):
    nseq, rows, w = qbd_m.shape
    n_pages = page_table.shape[1]
    nh, hd, page = ckt_m.shape[1:]
    pp = SAMPLE_PAGES_PER_STEP
    pages_per_block = MOBA_BLOCK // page
    assert n_pages % pp == 0 and pp % pages_per_block == 0
    n_steps = n_pages // pp
    nblk = n_pages // pages_per_block
    pt_flat = page_table.reshape(-1)

    seq = lambda shape: pl.BlockSpec((None,) + shape, lambda b, j, pt: (b,) + (0,) * len(shape))

    def pg(shape, p):
        return pl.BlockSpec(
            (None,) + shape,
            lambda b, j, pt: (pt[b * n_pages + n_pages - 1 - (j * pp + p)],) + (0,) * len(shape))

    page_specs, page_args = [], []
    for p in range(pp):
        page_specs += [pg((nh, hd, page), p)] * 4 + [pg((nh, page), p)] * 2
        page_args += [ckt_m, cvt_m, ckt_f, cvt_f, dec, tot]
    col = lambda: pltpu.VMEM((rows, 1), F32)
    stat = lambda: pltpu.VMEM((rows, LANES), F32)
    grid_spec = pltpu.PrefetchScalarGridSpec(
        num_scalar_prefetch=1,
        grid=(nseq, n_steps),
        in_specs=[seq((rows, w)), seq((rows, w)), seq((nh, hd, page)), seq((nh, hd, page)),
                  seq((nh, hd, page)), seq((nh, hd, page)), seq((rows, LANES))] + page_specs,
        out_specs=[seq((rows, w)), seq((rows, w))],
        scratch_shapes=[col(), col(), pltpu.VMEM((rows, w), F32), stat(),
                        stat(), stat(), stat(), pltpu.VMEM((nblk, rows, w), F32)])
    return pl.pallas_call(
        functools.partial(_sample_attn_kernel, n_steps=n_steps, rows=rows,
                          pages_per_block=pages_per_block),
        grid_spec=grid_spec,
        out_shape=[jax.ShapeDtypeStruct((nseq, rows, w), F32)] * 2,
        compiler_params=_cparams(("arbitrary", "arbitrary")),
        name="sample_attention",
    )(pt_flat, qbd_m, qbd_f, knew_m, vnew_m, knew_f, vnew_f, lnew, *page_args)


def _block_diag_queries(q, nseq, t):
    nh = N_GROUP_HEADS
    q4 = q.reshape(nseq, t, nh, HEAD_DIM)
    eye = jnp.eye(nh, dtype=q.dtype)
    return jnp.einsum("bthd,hg->bhtgd", q4, eye).reshape(nseq, nh * t, nh * HEAD_DIM)


def _diag_heads(o, nseq, t):
    nh = N_GROUP_HEADS
    o5 = o.reshape(nseq, nh, t, nh, HEAD_DIM)
    eye = jnp.eye(nh, dtype=o.dtype)
    return jnp.einsum("bhtgd,hg->bthd", o5, eye).reshape(nseq * t, nh * HEAD_DIM)


def _sample_mixers(x, mod, g_attn, w_in, b_forget, caches, page_table):
    nseq, t, d = x.shape
    nh = N_GROUP_HEADS
    n = nseq * t
    ck_m, cv_m, ck_f, cv_f, clf = caches
    page = ck_m.shape[1]
    past_len = page_table.shape[1] * page
    pos = jnp.tile(past_len + jnp.arange(t, dtype=jnp.int32), nseq)
    per_tok = lambda a: jnp.broadcast_to(a[:, None, :], (nseq, t, d)).reshape(n, d)
    qm, km, vm, qf, kf, vf, logf = _inproj_sample(
        x.reshape(n, d), per_tok(mod[:, 0]), per_tok(mod[:, 1]), g_attn, w_in, b_forget, pos)
    shp = (nseq, t, nh, HEAD_DIM)
    pages_t = lambda c: jnp.transpose(c, (0, 2, 3, 1))
    new_page = lambda a: jnp.pad(pages_t(a.reshape(shp)), ((0, 0), (0, 0), (0, 0), (0, page - t)))
    lf_new = logf[:, 0:nh].reshape(nseq, t, nh)
    lnew = jnp.pad(jnp.repeat(jnp.swapaxes(lf_new, 1, 2), t, axis=1), ((0, 0), (0, 0), (0, LANES - t)))
    dec, tot = _logf_suffix(jnp.swapaxes(clf, 1, 2))
    om, of = _sample_attention(
        page_table, _block_diag_queries(qm, nseq, t), _block_diag_queries(qf, nseq, t),
        new_page(km), new_page(vm), new_page(kf), new_page(vf), lnew,
        pages_t(ck_m), pages_t(cv_m), pages_t(ck_f), pages_t(cv_f), dec, tot)
    otm = _diag_heads(om, nseq, t).T[None]
    otf = _diag_heads(of, nseq, t).T[None]
    new_rows = (km.reshape(shp), vm.reshape(shp), kf.reshape(shp), vf.reshape(shp), lf_new)
    return otm, otf, new_rows


def _prompt_mixers(x, mod, g_attn, w_in, b_forget):
    b, s, d = x.shape
    (km, vm, kf, vf, logf, kpm, kpf, qtm, qtg, qtf, fqt, vtm, vtf, kmean) = _inproj_prompt(
        x, mod[:, 0:1], mod[:, 1:2], g_attn, w_in, b_forget, ts=MOBA_BLOCK)
    bias = _moba_gate(qtg, kmean, tg=min(s, 2048))
    otm = _attention(qtm, bias, kpm, vtm, MOBA_EXT_ROWS, MOBA_BLOCK)
    otf = _attention(qtf, fqt, kpf, vtf, FOX_EXT_ROWS, MOBA_BLOCK)
    nh = N_GROUP_HEADS
    new_rows = (km.reshape(b, s, nh, HEAD_DIM), vm.reshape(b, s, nh, HEAD_DIM),
                kf.reshape(b, s, nh, HEAD_DIM), vf.reshape(b, s, nh, HEAD_DIM), logf[:, :, 0:nh])
    return otm, otf, new_rows


def _peer_ffn(h2, x1, gate, peer_tables, g_final, tt_route, tt, te, tokens_per_gate):
    wqt, subkeys, u_bf, vt_bf = peer_tables
    route = _peer_route(h2, wqt, subkeys, tt_route)
    return _peer_experts(h2, u_bf, vt_bf, route, x1, gate, g_final, tt, te, tokens_per_gate)


def kernel(x_prompt, x_sample, c_prompt, c_sample, cache_k_moba, cache_v_moba, cache_k_fox, cache_v_fox, cache_logf_fox, page_table, w_ada, b_ada, g_attn, g_ffn, w_in, b_forget, g_out_moba, g_out_fox, w_out, peer_wq, peer_subkeys, peer_u, peer_v, g_final):
    b, s, d = x_prompt.shape
    nseq, t, _ = x_sample.shape
    depth = w_ada.shape[0]
    assert depth == 1 and t == SUBLANES and s % MOBA_BLOCK == 0 and s // MOBA_BLOCK <= MOBA_EXT_ROWS
    l = 0

    c_all = jnp.concatenate([c_prompt, c_sample], axis=0)
    pad = -c_all.shape[0] % SUBLANES
    mod = _modulation(jnp.pad(c_all, ((0, pad), (0, 0))), w_ada[l], b_ada[l])
    mod = mod[:b + nseq].reshape(b + nseq, N_ADA, d)
    mp, ms = mod[:b], mod[b:]

    peer_tables = (peer_wq[l].T.astype(BF16),
                   peer_subkeys[l].reshape(PEER_HEADS * 2, PEER_NKEYS, PEER_HALF),
                   peer_u[l].astype(BF16), peer_v[l].T.astype(BF16))

    otm, otf, rows_p = _prompt_mixers(x_prompt, mp, g_attn[l], w_in[l], b_forget[l])
    x1, h2 = _merge(x_prompt, otm, otf, mp[:, 2:3], mp[:, 3:4], mp[:, 4:5],
                    g_out_moba[l], g_out_fox[l], g_ffn[l], w_out[l], tm=min(s, 512))
    y_prompt = _peer_ffn(h2.reshape(b * s, d), x1.reshape(b * s, d), mp[:, 5:6], peer_tables, g_final,
                         tt_route=2 * LANES, tt=min(s, 512), te=1024, tokens_per_gate=s).reshape(b, s, d)

    n = nseq * t
    per_tok = lambda a: jnp.broadcast_to(a[:, None, :], (nseq, t, d)).reshape(1, n, d)
    caches = (cache_k_moba[l], cache_v_moba[l], cache_k_fox[l], cache_v_fox[l], cache_logf_fox[l])
    otm_s, otf_s, rows_s = _sample_mixers(x_sample, ms, g_attn[l], w_in[l], b_forget[l], caches, page_table)
    x1s, h2s = _merge(x_sample.reshape(1, n, d), otm_s, otf_s, per_tok(ms[:, 2]), per_tok(ms[:, 3]),
                      per_tok(ms[:, 4]), g_out_moba[l], g_out_fox[l], g_ffn[l], w_out[l], tm=n)
    y_sample = _peer_ffn(h2s.reshape(n, d), x1s.reshape(n, d), per_tok(ms[:, 5]), peer_tables, g_final,
                         tt_route=min(n, 2 * LANES), tt=n, te=1024, tokens_per_gate=n).reshape(nseq, t, d)

    stack = lambda a: a[None]
    return (y_prompt, y_sample) + tuple(stack(a) for a in rows_p) + tuple(stack(a) for a in rows_s)
```
